```python
import math
import jax, jax.numpy as jnp
from jax import lax
import numpy as np

D_MODEL = 1024
BATCH = 16
SEQ = 4096
DEPTH = 4

MEM_LEN = 256
N_EVEN = (DEPTH + 1) // 2
N_ODD = DEPTH // 2
EPS = 1e-6
ROPE_THETA = 10000.0
Q_BLOCK = 128

D_FF = 2816
CONV_DIM = 512
CONV_WIDTH = 31
MLA_HEADS = 8
Q_LORA = 256
KV_LORA = 128
QK_NOPE = 64
QK_ROPE = 32
V_HEAD = 64
MLA_OUT = MLA_HEADS * V_HEAD
EVEN_IN = 2 * CONV_DIM + Q_LORA + KV_LORA + QK_ROPE
EVEN_MIX = CONV_DIM + MLA_OUT
RET_HEADS = 4
RET_QK = 256
RET_V = 512
RET_CHUNK = 128
ODD_MIX = RET_HEADS * RET_V
ODD_IN = 2 * RET_HEADS * RET_QK + 2 * ODD_MIX
X_HEADS = 4
X_HEAD_DIM = D_MODEL // X_HEADS

kernel_name = 'hybrid_conv_mla_retention_trunk'


def rms_norm(x, g):
    xf = x.astype(jnp.float32)
    y = xf * lax.rsqrt(jnp.mean(xf * xf, axis=-1, keepdims=True) + EPS)
    return (y * g.astype(jnp.float32)).astype(x.dtype)


def layer_norm(x, g, b):
    xf = x.astype(jnp.float32)
    mu = jnp.mean(xf, axis=-1, keepdims=True)
    xc = xf - mu
    var = jnp.mean(xc * xc, axis=-1, keepdims=True)
    return (xc * lax.rsqrt(var + EPS) * g.astype(jnp.float32) + b.astype(jnp.float32)).astype(x.dtype)


def rope_tables(positions, dim):
    inv = ROPE_THETA ** (-jnp.arange(0, dim, 2, dtype=jnp.float32) / dim)
    ang = positions.astype(jnp.float32)[..., None] * inv
    return jnp.cos(ang), jnp.sin(ang)


def apply_rope(x, cos, sin):
    x1, x2 = jnp.split(x, 2, axis=-1)
    c = cos[:, :, None, :].astype(x.dtype)
    s = sin[:, :, None, :].astype(x.dtype)
    return jnp.concatenate([x1 * c - x2 * s, x1 * s + x2 * c], axis=-1)


def swiglu(h, w_gate, w_up, w_down):
    return (jax.nn.silu(h @ w_gate) * (h @ w_up)) @ w_down


def causal_depthwise_conv(a, w, b):
    y = lax.conv_general_dilated(
        a, w[:, None, :].astype(a.dtype), window_strides=(1,),
        padding=[(CONV_WIDTH - 1, 0)], dimension_numbers=('NWC', 'WIO', 'NWC'),
        feature_group_count=a.shape[-1])
    return y + b.astype(a.dtype)


def causal_block_attention(q, k, v, scale):
    B, S, H, Dq = q.shape
    nb = S // Q_BLOCK
    kf = k.astype(jnp.float32)
    vf = v.astype(jnp.float32)
    qb = q.astype(jnp.float32).reshape(B, nb, Q_BLOCK, H, Dq).transpose(1, 0, 2, 3, 4)
    kpos = jnp.arange(S)

    def one_block(args):
        qi, i = args
        s = jnp.einsum('bqhd,bkhd->bhqk', qi, kf) * scale
        qpos = i * Q_BLOCK + jnp.arange(Q_BLOCK)
        s = jnp.where(kpos[None, :] <= qpos[:, None], s, -jnp.inf)
        p = jax.nn.softmax(s, axis=-1)
        return jnp.einsum('bhqk,bkhd->bqhd', p, vf)

    out = lax.map(one_block, (qb, jnp.arange(nb)))
    return out.transpose(1, 0, 2, 3, 4).reshape(B, S, H, v.shape[-1]).astype(v.dtype)


def mla_heads(z_q, z_kv, z_kr, cos, sin, q_a_norm, w_q_b, kv_a_norm, w_kv_b,
              q_nope_norm, k_nope_norm, q_rope_norm, k_rope_norm):
    B, S, _ = z_q.shape
    q = (rms_norm(z_q, q_a_norm) @ w_q_b).reshape(B, S, MLA_HEADS, QK_NOPE + QK_ROPE)
    kv = (rms_norm(z_kv, kv_a_norm) @ w_kv_b).reshape(B, S, MLA_HEADS, QK_NOPE + V_HEAD)
    q_nope, q_rope = jnp.split(q, [QK_NOPE], axis=-1)
    k_nope, v = jnp.split(kv, [QK_NOPE], axis=-1)
    q_nope = rms_norm(q_nope, q_nope_norm)
    k_nope = rms_norm(k_nope, k_nope_norm)
    q_rope = apply_rope(rms_norm(q_rope, q_rope_norm), cos, sin)
    k_rope = apply_rope(rms_norm(z_kr[:, :, None, :], k_rope_norm), cos, sin)
    qf = jnp.concatenate([q_nope, q_rope], axis=-1)
    kf = jnp.concatenate([k_nope, jnp.broadcast_to(k_rope, (B, S, MLA_HEADS, QK_ROPE))], axis=-1)
    o = causal_block_attention(qf, kf, v, (QK_NOPE + QK_ROPE) ** -0.5)
    return o.reshape(B, S, MLA_OUT)


def conv_mla_mixer(h, cos_m, sin_m, w_in, conv_w, conv_b, conv_ln_g, conv_ln_b,
                   q_a_norm, w_q_b, kv_a_norm, w_kv_b, q_nope_norm, k_nope_norm,
                   q_rope_norm, k_rope_norm, w_out):
    z = h @ w_in
    z_val, z_gate, z_q, z_kv, z_kr = jnp.split(
        z, [CONV_DIM, 2 * CONV_DIM, 2 * CONV_DIM + Q_LORA, 2 * CONV_DIM + Q_LORA + KV_LORA], axis=-1)
    a = z_val * jax.nn.sigmoid(z_gate)
    a = jax.nn.silu(layer_norm(causal_depthwise_conv(a, conv_w, conv_b), conv_ln_g, conv_ln_b))
    m = mla_heads(z_q, z_kv, z_kr, cos_m, sin_m, q_a_norm, w_q_b, kv_a_norm, w_kv_b,
                  q_nope_norm, k_nope_norm, q_rope_norm, k_rope_norm)
    return jnp.concatenate([a, m], axis=-1) @ w_out


def retention_chunkwise(q, k, v):
    B, S, H, Dk = q.shape
    Dv = v.shape[-1]
    n = S // RET_CHUNK
    log_g = jnp.log1p(-jnp.exp2(-5.0 - jnp.arange(H, dtype=jnp.float32)))
    idx = jnp.arange(RET_CHUNK, dtype=jnp.float32)
    diff = idx[:, None] - idx[None, :]
    decay = jnp.where(diff >= 0, jnp.exp(log_g[:, None, None] * jnp.maximum(diff, 0.0)), 0.0)
    xi = jnp.exp(log_g[:, None] * (idx + 1.0))[None, :, :, None]
    zeta = jnp.exp(log_g[:, None] * (RET_CHUNK - 1.0 - idx))[None, :, :, None]
    g_chunk = jnp.exp(log_g * RET_CHUNK)[None, :, None, None]

    def to_chunks(t):
        return t.astype(jnp.float32).reshape(B, n, RET_CHUNK, H, t.shape[-1]).transpose(1, 0, 3, 2, 4)

    def step(state, inp):
        qc, kc, vc = inp
        scores = jnp.einsum('bhid,bhjd->bhij', qc, kc) * decay
        out = (jnp.einsum('bhij,bhjv->bhiv', scores, vc)
               + jnp.einsum('bhid,bhdv->bhiv', qc, state) * xi)
        state = state * g_chunk + jnp.einsum('bhjd,bhjv->bhdv', kc * zeta, vc)
        return state, out

    state0 = jnp.zeros((B, H, Dk, Dv), jnp.float32)
    _, out = lax.scan(step, state0, (to_chunks(q), to_chunks(k), to_chunks(v)))
    return out.transpose(1, 0, 3, 2, 4).reshape(B, S, H, Dv)


def retention_mixer(h, cos_r, sin_r, w_in, gn_g, gn_b, w_out):
    B, S, _ = h.shape
    z = h @ w_in
    q, k, v, g = jnp.split(z, [RET_HEADS * RET_QK, 2 * RET_HEADS * RET_QK,
                               2 * RET_HEADS * RET_QK + ODD_MIX], axis=-1)
    q = apply_rope(q.reshape(B, S, RET_HEADS, RET_QK), cos_r, sin_r)
    k = apply_rope(k.reshape(B, S, RET_HEADS, RET_QK), cos_r, sin_r) * (RET_QK ** -0.5)
    v = v.reshape(B, S, RET_HEADS, RET_V)
    o = retention_chunkwise(q, k, v).astype(h.dtype)
    o = layer_norm(o, gn_g.reshape(RET_HEADS, RET_V), gn_b.reshape(RET_HEADS, RET_V))
    return (jax.nn.silu(g) * o.reshape(B, S, ODD_MIX)) @ w_out


def memory_cross_attention(h, m, wq, wk, wv, wo, q_norm, k_norm):
    B, S, _ = h.shape
    M = m.shape[1]
    q = rms_norm((h @ wq).reshape(B, S, X_HEADS, X_HEAD_DIM), q_norm)
    k = rms_norm((m @ wk).reshape(B, M, X_HEADS, X_HEAD_DIM), k_norm)
    v = (m @ wv).reshape(B, M, X_HEADS, X_HEAD_DIM)
    s = jnp.einsum('bshd,bmhd->bhsm', q.astype(jnp.float32), k.astype(jnp.float32)) * (X_HEAD_DIM ** -0.5)
    p = jax.nn.softmax(s, axis=-1)
    o = jnp.einsum('bhsm,bmhd->bshd', p, v.astype(jnp.float32)).astype(h.dtype)
    return o.reshape(B, S, D_MODEL) @ wo


def setup_inputs(seed: int = 0) -> dict:
    key = jax.random.key(seed)
    ks = iter(jax.random.split(key, 64))

    def w(shape, fan_in):
        return jax.random.normal(next(ks), shape, jnp.float32) * (fan_in ** -0.5)

    def gain(shape):
        return 1.0 + 0.02 * jax.random.normal(next(ks), shape, jnp.float32)

    def bias(shape):
        return 0.01 * jax.random.normal(next(ks), shape, jnp.float32)

    L, E, O, D = DEPTH, N_EVEN, N_ODD, D_MODEL
    x = jax.random.normal(next(ks), (BATCH, SEQ, D), jnp.float32)
    mem = jax.random.normal(next(ks), (BATCH, MEM_LEN, D), jnp.float32)
    offset = jax.random.randint(next(ks), (BATCH, 1), 0, 1024, dtype=jnp.int32)
    positions = jnp.arange(SEQ, dtype=jnp.int32)[None, :] + offset
    return {
        'x': x, 'mem': mem, 'positions': positions,
        'ffn1_norm': gain((L, D)), 'ffn1_w_gate': w((L, D, D_FF), D),
        'ffn1_w_up': w((L, D, D_FF), D), 'ffn1_w_down': w((L, D_FF, D), D_FF),
        'ffn2_norm': gain((L, D)), 'ffn2_w_gate': w((L, D, D_FF), D),
        'ffn2_w_up': w((L, D, D_FF), D), 'ffn2_w_down': w((L, D_FF, D), D_FF),
        'mix_norm': gain((L, D)), 'xattn_norm': gain((L, D)), 'mem_norm': gain((L, D)),
        'xattn_wq': w((L, D, D), D), 'xattn_wk': w((L, D, D), D),
        'xattn_wv': w((L, D, D), D), 'xattn_wo': w((L, D, D), D),
        'xattn_q_norm': gain((L, X_HEAD_DIM)), 'xattn_k_norm': gain((L, X_HEAD_DIM)),
        'ev_w_in': w((E, D, EVEN_IN), D),
        'ev_conv_w': w((E, CONV_WIDTH, CONV_DIM), CONV_WIDTH), 'ev_conv_b': bias((E, CONV_DIM)),
        'ev_conv_ln_g': gain((E, CONV_DIM)), 'ev_conv_ln_b': bias((E, CONV_DIM)),
        'ev_q_a_norm': gain((E, Q_LORA)), 'ev_w_q_b': w((E, Q_LORA, MLA_HEADS * (QK_NOPE + QK_ROPE)), Q_LORA),
        'ev_kv_a_norm': gain((E, KV_LORA)), 'ev_w_kv_b': w((E, KV_LORA, MLA_HEADS * (QK_NOPE + V_HEAD)), KV_LORA),
        'ev_q_nope_norm': gain((E, QK_NOPE)), 'ev_k_nope_norm': gain((E, QK_NOPE)),
        'ev_q_rope_norm': gain((E, QK_ROPE)), 'ev_k_rope_norm': gain((E, QK_ROPE)),
        'ev_w_out': w((E, EVEN_MIX, D), EVEN_MIX),
        'od_w_in': w((O, D, ODD_IN), D), 'od_gn_g': gain((O, ODD_MIX)), 'od_gn_b': bias((O, ODD_MIX)),
        'od_w_out': w((O, ODD_MIX, D), ODD_MIX),
    }


def reference(x, mem, positions,
              ffn1_norm, ffn1_w_gate, ffn1_w_up, ffn1_w_down,
              ffn2_norm, ffn2_w_gate, ffn2_w_up, ffn2_w_down,
              mix_norm, xattn_norm, mem_norm,
              xattn_wq, xattn_wk, xattn_wv, xattn_wo, xattn_q_norm, xattn_k_norm,
              ev_w_in, ev_conv_w, ev_conv_b, ev_conv_ln_g, ev_conv_ln_b,
              ev_q_a_norm, ev_w_q_b, ev_kv_a_norm, ev_w_kv_b,
              ev_q_nope_norm, ev_k_nope_norm, ev_q_rope_norm, ev_k_rope_norm, ev_w_out,
              od_w_in, od_gn_g, od_gn_b, od_w_out):
    cos_m, sin_m = rope_tables(positions, QK_ROPE)
    cos_r, sin_r = rope_tables(positions, RET_QK)
    for l in range(DEPTH):
        x = x + 0.5 * swiglu(rms_norm(x, ffn1_norm[l]), ffn1_w_gate[l], ffn1_w_up[l], ffn1_w_down[l])
        h = rms_norm(x, mix_norm[l])
        if l % 2 == 0:
            e = l // 2
            x = x + conv_mla_mixer(h, cos_m, sin_m, ev_w_in[e], ev_conv_w[e], ev_conv_b[e],
                                   ev_conv_ln_g[e], ev_conv_ln_b[e], ev_q_a_norm[e], ev_w_q_b[e],
                                   ev_kv_a_norm[e], ev_w_kv_b[e], ev_q_nope_norm[e], ev_k_nope_norm[e],
                                   ev_q_rope_norm[e], ev_k_rope_norm[e], ev_w_out[e])
        else:
            o = l // 2
            x = x + retention_mixer(h, cos_r, sin_r, od_w_in[o], od_gn_g[o], od_gn_b[o], od_w_out[o])
        x = x + memory_cross_attention(rms_norm(x, xattn_norm[l]), rms_norm(mem, mem_norm[l]),
                                       xattn_wq[l], xattn_wk[l], xattn_wv[l], xattn_wo[l],
                                       xattn_q_norm[l], xattn_k_norm[l])
        x = x + 0.5 * swiglu(rms_norm(x, ffn2_norm[l]), ffn2_w_gate[l], ffn2_w_up[l], ffn2_w_down[l])
    return x
```

```python
import functools
import math

import jax
import jax.numpy as jnp
from jax import lax
from jax.experimental import pallas as pl
from jax.experimental.pallas import tpu as pltpu

F32 = jnp.float32
BF16 = jnp.bfloat16

EPS = 1e-6
ROPE_THETA = 10000.0

D_FF = 2816
CONV_DIM = 512
CONV_WIDTH = 31
MLA_HEADS = 8
Q_LORA = 256
KV_LORA = 128
QK_NOPE = 64
QK_ROPE = 32
V_HEAD = 64
RET_HEADS = 4
RET_QK = 256
RET_V = 512
RET_CHUNK = 128
X_HEADS = 4

V7X_LANES = 128
V7X_MXU_DIM = 256
V7X_VMEM_LIMIT_BYTES = 56 * 1024 * 1024

TOKEN_TILE = 512
CONV_HALO = 32
MLA_HEAD_PAD = 128


def _params(*sem):
    return pltpu.CompilerParams(dimension_semantics=sem, vmem_limit_bytes=V7X_VMEM_LIMIT_BYTES)


def _const_spec(shape):
    nd = len(shape)
    return pl.BlockSpec(shape, lambda *_: (0,) * nd, pipeline_mode=pl.Buffered(1))


def _rms(x, gain):
    return x * lax.rsqrt(jnp.mean(x * x, axis=-1, keepdims=True) + EPS) * gain


def _silu(x):
    return x / (1.0 + jnp.exp(-x))


def _dot(a, b):
    return jnp.dot(a, b, preferred_element_type=F32)


def _dot_nt(a, b):
    return lax.dot_general(a, b, (((1,), (1,)), ((), ())), preferred_element_type=F32)


def _dot_tn(a, b):
    return lax.dot_general(a, b, (((0,), (0,)), ((), ())), preferred_element_type=F32)


def _rope_kernel(pos_ref, inv_r_ref, inv_m_ref, sgn_m_ref, cos_r_ref, sin_r_ref, cm_ref, sm_ref):
    pos = pos_ref[...].astype(F32)
    ang_r = pos * inv_r_ref[...]
    cos_r_ref[...] = jnp.cos(ang_r)
    sin_r_ref[...] = jnp.sin(ang_r)
    ang_m = pos * inv_m_ref[...]
    cm_ref[...] = jnp.cos(ang_m)
    sm_ref[...] = jnp.sin(ang_m) * sgn_m_ref[...]


def _rope_tables(positions, tm):
    T = positions.size
    pos = positions.reshape(T, 1)
    inv_r = (ROPE_THETA ** (-jnp.arange(0, RET_QK, 2, dtype=F32) / RET_QK)).reshape(1, RET_QK // 2)
    inv_m16 = ROPE_THETA ** (-jnp.arange(0, QK_ROPE, 2, dtype=F32) / QK_ROPE)
    half = QK_ROPE // 2
    pad = MLA_HEAD_PAD - QK_NOPE - QK_ROPE
    inv_m = jnp.concatenate([jnp.zeros((QK_NOPE,), F32), inv_m16, inv_m16, jnp.zeros((pad,), F32)])
    sgn_m = jnp.concatenate([jnp.zeros((QK_NOPE,), F32), -jnp.ones((half,), F32),
                             jnp.ones((half,), F32), jnp.zeros((pad,), F32)])
    row = pl.BlockSpec((tm, V7X_LANES), lambda i: (i, 0))
    vec = _const_spec((1, V7X_LANES))
    out = jax.ShapeDtypeStruct((T, V7X_LANES), F32)
    return pl.pallas_call(
        _rope_kernel,
        grid=(T // tm,),
        in_specs=[pl.BlockSpec((tm, 1), lambda i: (i, 0)), vec, vec, vec],
        out_specs=[row, row, row, row],
        out_shape=[out, out, out, out],
        compiler_params=_params("parallel"),
        name="rope_tables",
    )(pos, inv_r, inv_m.reshape(1, -1), sgn_m.reshape(1, -1))


def _ff_chunks(d_ff, width):
    edges = list(range(0, d_ff, width)) + [d_ff]
    return [(a, b) for a, b in zip(edges[:-1], edges[1:])]


def _ffn_kernel(x_ref, g_ref, wg_ref, wu_ref, wd_ref, o_ref, h_ref):
    x = x_ref[...]
    xn = _rms(x, g_ref[...]).astype(BF16)
    for a, b in _ff_chunks(h_ref.shape[1], 2 * V7X_MXU_DIM):
        gate = _dot(xn, wg_ref[:, a:b])
        up = _dot(xn, wu_ref[:, a:b])
        h_ref[:, a:b] = (_silu(gate) * up).astype(BF16)
    o_ref[...] = x + 0.5 * _dot(h_ref[...], wd_ref[...])


def _ffn(x, gain, w_gate, w_up, w_down, tm):
    T, D = x.shape
    d_ff = w_gate.shape[1]
    row = pl.BlockSpec((tm, D), lambda i: (i, 0))
    return pl.pallas_call(
        _ffn_kernel,
        grid=(T // tm,),
        in_specs=[row, _const_spec((1, D)), _const_spec((D, d_ff)), _const_spec((D, d_ff)),
                  _const_spec((d_ff, D))],
        out_specs=row,
        out_shape=jax.ShapeDtypeStruct((T, D), F32),
        scratch_shapes=[pltpu.VMEM((tm, d_ff), BF16)],
        compiler_params=_params("parallel"),
        name="ffn",
    )(x, gain.reshape(1, D), w_gate, w_up, w_down)


def _memkv_kernel(mem_ref, g_ref, wk_ref, wv_ref, kn_ref, kt_ref, v_ref):
    mn = _rms(mem_ref[...], g_ref[...]).astype(BF16)
    k = _dot(mn, wk_ref[...])
    hd = kn_ref.shape[1]
    for h in range(X_HEADS):
        kh = _rms(k[:, h * hd:(h + 1) * hd], kn_ref[...])
        kt_ref[0, h * hd:(h + 1) * hd, :] = kh.T.astype(BF16)
    v_ref[0] = _dot(mn, wv_ref[...]).astype(BF16)


def _memkv(mem, gain, wk, wv, k_norm):
    B, M, D = mem.shape
    hd = D // X_HEADS
    return pl.pallas_call(
        _memkv_kernel,
        grid=(B,),
        in_specs=[pl.BlockSpec((M, D), lambda b: (b, 0)), _const_spec((1, D)),
                  _const_spec((D, D)), _const_spec((D, D)), _const_spec((1, hd))],
        out_specs=[pl.BlockSpec((1, D, M), lambda b: (b, 0, 0)),
                   pl.BlockSpec((1, M, D), lambda b: (b, 0, 0))],
        out_shape=[jax.ShapeDtypeStruct((B, D, M), BF16), jax.ShapeDtypeStruct((B, M, D), BF16)],
        compiler_params=_params("parallel"),
        name="xattn_memkv",
    )(mem.reshape(B * M, D), gain.reshape(1, D), wk, wv, k_norm.reshape(1, hd))


def _xattn_kernel(x_ref, g_ref, wq_ref, qn_ref, kt_ref, v_ref, wo_ref, o_ref, att_ref):
    x = x_ref[...]
    xn = _rms(x, g_ref[...]).astype(BF16)
    q = _dot(xn, wq_ref[...])
    hd = qn_ref.shape[1]
    scale = hd ** -0.5
    for h in range(X_HEADS):
        sl = slice(h * hd, (h + 1) * hd)
        qh = (_rms(q[:, sl], qn_ref[...]) * scale).astype(BF16)
        s = _dot(qh, kt_ref[0, sl, :])
        p = jnp.exp(s - jnp.max(s, axis=-1, keepdims=True))
        l = jnp.sum(p, axis=-1, keepdims=True)
        att_ref[:, sl] = (_dot(p.astype(BF16), v_ref[0, :, sl]) / l).astype(BF16)
    o_ref[...] = x + _dot(att_ref[...], wo_ref[...])


def _xattn(x, gain, wq, q_norm, kt, v, wo, tm, seq):
    T, D = x.shape
    M = v.shape[1]
    hd = D // X_HEADS
    per_seq = seq // tm
    row = pl.BlockSpec((tm, D), lambda i: (i, 0))
    return pl.pallas_call(
        _xattn_kernel,
        grid=(T // tm,),
        in_specs=[row, _const_spec((1, D)), _const_spec((D, D)), _const_spec((1, hd)),
                  pl.BlockSpec((1, D, M), lambda i: (i // per_seq, 0, 0)),
                  pl.BlockSpec((1, M, D), lambda i: (i // per_seq, 0, 0)),
                  _const_spec((D, D))],
        out_specs=row,
        out_shape=jax.ShapeDtypeStruct((T, D), F32),
        scratch_shapes=[pltpu.VMEM((tm, D), BF16)],
        compiler_params=_params("parallel"),
        name="xattn",
    )(x, gain.reshape(1, D), wq, q_norm.reshape(1, hd), kt, v, wo)


def _even_pre_kernel(x_ref, g_ref, win_ref, cm_ref, sm_ref, qa_ref, wq_ref, kva_ref, wk_ref, wv_ref,
                     seg_ref, gq_ref, gqs_ref, gk_ref, gkr_ref, gkrs_ref,
                     a_ref, q_ref, k_ref, v_ref):
    xn = _rms(x_ref[...], g_ref[...]).astype(BF16)
    c = CONV_DIM
    a_ref[...] = _dot(xn, win_ref[:, 0:c]) / (1.0 + jnp.exp(-_dot(xn, win_ref[:, c:2 * c])))
    o = 2 * c
    zq = _dot(xn, win_ref[:, o:o + Q_LORA])
    o += Q_LORA
    zkv = _dot(xn, win_ref[:, o:o + KV_LORA])
    o += KV_LORA
    zkr = _dot(xn, win_ref[:, o:o + MLA_HEAD_PAD])
    o += MLA_HEAD_PAD
    zkr_sw = _dot(xn, win_ref[:, o:o + MLA_HEAD_PAD])
    cm = cm_ref[...]
    sm = sm_ref[...]

    rs_kr = lax.rsqrt(jnp.sum(zkr * zkr, axis=-1, keepdims=True) * (1.0 / QK_ROPE) + EPS)
    k_rope = (zkr * rs_kr * gkr_ref[...]) * cm + (zkr_sw * rs_kr * gkrs_ref[...]) * sm

    zq_n = _rms(zq, qa_ref[...]).astype(BF16)
    zkv_n = _rms(zkv, kva_ref[...]).astype(BF16)
    v_ref[...] = _dot(zkv_n, wv_ref[...]).astype(BF16)
    hp = MLA_HEAD_PAD
    nq = MLA_HEADS * hp
    seg = seg_ref[...]
    for h in range(MLA_HEADS):
        sl = slice(h * hp, (h + 1) * hp)
        qh = _dot(zq_n, wq_ref[:, sl])
        qh_sw = _dot(zq_n, wq_ref[:, nq + h * hp:nq + (h + 1) * hp])
        rs = lax.rsqrt(_dot((qh * qh).astype(BF16), seg) + EPS)
        q_ref[:, sl] = ((qh * rs * gq_ref[...]) * cm + (qh_sw * rs * gqs_ref[...]) * sm).astype(BF16)
        kh = _dot(zkv_n, wk_ref[:, sl])
        rk = lax.rsqrt(_dot((kh * kh).astype(BF16), seg) + EPS)
        k_ref[:, sl] = (kh * rk * gk_ref[...] + k_rope).astype(BF16)


def _even_pre(x, gain, w_in_p, cm, sm, qa, wq_p, kva, wk_p, wv_p, seg, gq, gqs, gk, gkr, gkrs, tm):
    T, D = x.shape
    row = lambda w: pl.BlockSpec((tm, w), lambda i: (i, 0))
    hq = MLA_HEADS * MLA_HEAD_PAD
    consts = [gain.reshape(1, D), w_in_p]
    tables = [cm, sm]
    rest = [qa, wq_p, kva, wk_p, wv_p, seg, gq, gqs, gk, gkr, gkrs]
    return pl.pallas_call(
        _even_pre_kernel,
        grid=(T // tm,),
        in_specs=[row(D)] + [_const_spec(a.shape) for a in consts] + [row(V7X_LANES)] * 2
                 + [_const_spec(a.shape) for a in rest],
        out_specs=[row(CONV_DIM), row(hq), row(hq), row(MLA_HEADS * V_HEAD)],
        out_shape=[jax.ShapeDtypeStruct((T, CONV_DIM), F32), jax.ShapeDtypeStruct((T, hq), BF16),
                   jax.ShapeDtypeStruct((T, hq), BF16),
                   jax.ShapeDtypeStruct((T, MLA_HEADS * V_HEAD), BF16)],
        compiler_params=_params("parallel"),
        name="even_pre",
    )(x, *consts, *tables, *rest)


MLA_HEADS_PER_STEP = 2


def _mla_kernel(q_ref, k_ref, v_ref, o_ref, m_ref, l_ref, acc_ref):
    qi = pl.program_id(2)
    tq = q_ref.shape[0]
    hp = MLA_HEAD_PAD
    scale = (QK_NOPE + QK_ROPE) ** -0.5

    m_ref[...] = jnp.full(m_ref.shape, -jnp.inf, F32)
    l_ref[...] = jnp.zeros(l_ref.shape, F32)
    acc_ref[...] = jnp.zeros(acc_ref.shape, F32)

    def block(j, masked):
        rows = pl.ds(pl.multiple_of(j * tq, tq), tq)
        for h in range(MLA_HEADS_PER_STEP):
            s = _dot_nt(q_ref[:, h * hp:(h + 1) * hp], k_ref[rows, h * hp:(h + 1) * hp]) * scale
            if masked:
                r = lax.broadcasted_iota(jnp.int32, s.shape, 0)
                c = lax.broadcasted_iota(jnp.int32, s.shape, 1)
                s = jnp.where(c <= r, s, -jnp.inf)
            m_old = m_ref[h]
            m_new = jnp.maximum(m_old, jnp.max(s, axis=-1, keepdims=True))
            alpha = jnp.exp(m_old - m_new)
            p = jnp.exp(s - m_new)
            l_ref[h] = alpha * l_ref[h] + jnp.sum(p, axis=-1, keepdims=True)
            acc_ref[h] = alpha * acc_ref[h] + _dot(p.astype(BF16), v_ref[rows, h * V_HEAD:(h + 1) * V_HEAD])
            m_ref[h] = m_new

    def body(j, carry):
        block(j, False)
        return carry

    lax.fori_loop(0, qi, body, 0)
    block(qi, True)
    for h in range(MLA_HEADS_PER_STEP):
        o_ref[:, h * V_HEAD:(h + 1) * V_HEAD] = (acc_ref[h] / l_ref[h]).astype(BF16)


def _mla_attention(q, k, v, batch, seq, tq):
    T = q.shape[0]
    hs = MLA_HEADS_PER_STEP
    nq = seq // tq
    qw, vw = hs * MLA_HEAD_PAD, hs * V_HEAD
    return pl.pallas_call(
        _mla_kernel,
        grid=(batch, MLA_HEADS // hs, nq),
        in_specs=[pl.BlockSpec((tq, qw), lambda b, h, i: (b * nq + i, h)),
                  pl.BlockSpec((seq, qw), lambda b, h, i: (b, h)),
                  pl.BlockSpec((seq, vw), lambda b, h, i: (b, h))],
        out_specs=pl.BlockSpec((tq, vw), lambda b, h, i: (b * nq + i, h)),
        out_shape=jax.ShapeDtypeStruct((T, MLA_HEADS * V_HEAD), BF16),
        scratch_shapes=[pltpu.VMEM((hs, tq, 1), F32), pltpu.VMEM((hs, tq, 1), F32),
                        pltpu.VMEM((hs, tq, V_HEAD), F32)],
        compiler_params=_params("parallel", "parallel", "arbitrary"),
        name="mla_attention",
    )(q, k, v)


def _even_post_kernel(x_ref, halo_ref, a_ref, m_ref, cw_ref, cb_ref, lg_ref, lb_ref, wa_ref, wm_ref,
                      o_ref, win_ref, *, per_seq):
    ts = a_ref.shape[0]
    first = (pl.program_id(0) % per_seq) == 0
    win_ref[0:CONV_HALO, :] = jnp.where(first, 0.0, halo_ref[...])
    win_ref[CONV_HALO:, :] = a_ref[...]
    off = CONV_HALO - (CONV_WIDTH - 1)
    y = cb_ref[...] + cw_ref[0:1, :] * win_ref[off:off + ts, :]
    for j in range(1, CONV_WIDTH):
        y = y + cw_ref[j:j + 1, :] * win_ref[off + j:off + j + ts, :]
    mu = jnp.mean(y, axis=-1, keepdims=True)
    yc = y - mu
    var = jnp.mean(yc * yc, axis=-1, keepdims=True)
    act = _silu(yc * lax.rsqrt(var + EPS) * lg_ref[...] + lb_ref[...]).astype(BF16)
    o_ref[...] = x_ref[...] + _dot(act, wa_ref[...]) + _dot(m_ref[...], wm_ref[...])


def _even_post(x, a, m, conv_w, conv_b, ln_g, ln_b, w_out_a, w_out_m, ts, seq):
    T, D = x.shape
    C = a.shape[1]
    per_seq = seq // ts
    halo_blocks = ts // CONV_HALO
    row = lambda w: pl.BlockSpec((ts, w), lambda i: (i, 0))
    return pl.pallas_call(
        functools.partial(_even_post_kernel, per_seq=per_seq),
        grid=(T // ts,),
        in_specs=[row(D),
                  pl.BlockSpec((CONV_HALO, C), lambda i: (jnp.maximum(i * halo_blocks - 1, 0), 0)),
                  row(C), row(C), _const_spec((CONV_WIDTH, C)), _const_spec((1, C)),
                  _const_spec((1, C)), _const_spec((1, C)), _const_spec((C, D)), _const_spec((C, D))],
        out_specs=row(D),
        out_shape=jax.ShapeDtypeStruct((T, D), F32),
        scratch_shapes=[pltpu.VMEM((CONV_HALO + ts, C), F32)],
        compiler_params=_params("parallel"),
        name="even_post",
    )(x, a, a, m, conv_w, conv_b.reshape(1, C), ln_g.reshape(1, C), ln_b.reshape(1, C),
      w_out_a, w_out_m)


def _odd_pre_kernel(x_ref, g_ref, w_ref, cos_ref, sin_ref, q_ref, k_ref, v_ref, gate_ref):
    xn = _rms(x_ref[...], g_ref[...]).astype(BF16)
    cos = cos_ref[...]
    sin = sin_ref[...]
    half = RET_QK // 2
    nqk = RET_HEADS * RET_QK
    k_scale = RET_QK ** -0.5
    for h in range(RET_HEADS):
        for ref, base, scale in ((q_ref, 0, 1.0), (k_ref, nqk, k_scale)):
            lo = h * RET_QK
            z = _dot(xn, w_ref[:, base + lo:base + lo + RET_QK])
            x1, x2 = z[:, :half], z[:, half:]
            ref[:, lo:lo + half] = ((x1 * cos - x2 * sin) * scale).astype(BF16)
            ref[:, lo + half:lo + RET_QK] = ((x1 * sin + x2 * cos) * scale).astype(BF16)
    nv = RET_HEADS * RET_V
    for h in range(RET_HEADS):
        sl = slice(h * RET_V, (h + 1) * RET_V)
        v_ref[:, sl] = _dot(xn, w_ref[:, 2 * nqk + h * RET_V:2 * nqk + (h + 1) * RET_V]).astype(BF16)
        gate_ref[:, sl] = _dot(xn, w_ref[:, 2 * nqk + nv + h * RET_V:2 * nqk + nv + (h + 1) * RET_V])


def _odd_pre(x, gain, w_in, cos_r, sin_r, tm):
    T, D = x.shape
    nqk, nv = RET_HEADS * RET_QK, RET_HEADS * RET_V
    row = lambda w: pl.BlockSpec((tm, w), lambda i: (i, 0))
    return pl.pallas_call(
        _odd_pre_kernel,
        grid=(T // tm,),
        in_specs=[row(D), _const_spec((1, D)), _const_spec(w_in.shape), row(V7X_LANES), row(V7X_LANES)],
        out_specs=[row(nqk), row(nqk), row(nv), row(nv)],
        out_shape=[jax.ShapeDtypeStruct((T, nqk), BF16), jax.ShapeDtypeStruct((T, nqk), BF16),
                   jax.ShapeDtypeStruct((T, nv), BF16), jax.ShapeDtypeStruct((T, nv), F32)],
        compiler_params=_params("parallel"),
        name="odd_pre",
    )(x, gain.reshape(1, D), w_in, cos_r, sin_r)


def _retention_kernel(x_ref, q_ref, k_ref, v_ref, gate_ref, decay_ref, xi_ref, zeta_ref, gch_ref,
                      gng_ref, gnb_ref, wo_ref, o_ref, state_ref, y_ref):
    @pl.when(pl.program_id(1) == 0)
    def _():
        state_ref[...] = jnp.zeros(state_ref.shape, F32)

    ts = x_ref.shape[0]
    C = RET_CHUNK
    for c in range(ts // C):
        rows = slice(c * C, (c + 1) * C)
        for h in range(RET_HEADS):
            qk = slice(h * RET_QK, (h + 1) * RET_QK)
            vv = slice(h * RET_V, (h + 1) * RET_V)
            qc = q_ref[rows, qk]
            kc = k_ref[rows, qk]
            vc = v_ref[rows, vv]
            state = state_ref[h]
            scores = (_dot_nt(qc, kc) * decay_ref[h]).astype(BF16)
            out = _dot(scores, vc) + _dot(qc, state.astype(BF16)) * xi_ref[h]
            kz = (kc.astype(F32) * zeta_ref[h]).astype(BF16)
            state_ref[h] = state * gch_ref[h] + _dot_tn(kz, vc)
            mu = jnp.mean(out, axis=-1, keepdims=True)
            oc = out - mu
            var = jnp.mean(oc * oc, axis=-1, keepdims=True)
            gn = oc * lax.rsqrt(var + EPS) * gng_ref[:, vv] + gnb_ref[:, vv]
            y_ref[rows, vv] = (_silu(gate_ref[rows, vv]) * gn).astype(BF16)
    o_ref[...] = x_ref[...] + _dot(y_ref[...], wo_ref[...])


def _retention(x, q, k, v, gate, decay, xi, zeta, gch, gn_g, gn_b, w_out, batch, seq, ts):
    T, D = x.shape
    nqk, nv = RET_HEADS * RET_QK, RET_HEADS * RET_V
    per_seq = seq // ts
    row = lambda w: pl.BlockSpec((ts, w), lambda b, i: (b * per_seq + i, 0))
    return pl.pallas_call(
        _retention_kernel,
        grid=(batch, per_seq),
        in_specs=[row(D), row(nqk), row(nqk), row(nv), row(nv),
                  _const_spec(decay.shape), _const_spec(xi.shape), _const_spec(zeta.shape),
                  _const_spec(gch.shape), _const_spec((1, nv)), _const_spec((1, nv)),
                  _const_spec((nv, D))],
        out_specs=row(D),
        out_shape=jax.ShapeDtypeStruct((T, D), F32),
        scratch_shapes=[pltpu.VMEM((RET_HEADS, RET_QK, RET_V), F32), pltpu.VMEM((ts, nv), BF16)],
        compiler_params=_params("parallel", "arbitrary"),
        name="retention",
    )(x, q, k, v, gate, decay, xi, zeta, gch, gn_g.reshape(1, nv), gn_b.reshape(1, nv), w_out)


def _retention_constants():
    H, C = RET_HEADS, RET_CHUNK
    log_g = jnp.log1p(-jnp.exp2(-5.0 - jnp.arange(H, dtype=F32)))
    idx = jnp.arange(C, dtype=F32)
    diff = idx[:, None] - idx[None, :]
    decay = jnp.where(diff >= 0, jnp.exp(log_g[:, None, None] * jnp.maximum(diff, 0.0)), 0.0)
    xi = jnp.exp(log_g[:, None] * (idx + 1.0))[:, :, None]
    zeta = jnp.exp(log_g[:, None] * (C - 1.0 - idx))[:, :, None]
    gch = jnp.exp(log_g * C)[:, None, None]
    return decay, xi, zeta, gch


def _swap_halves(w):
    half = w.shape[-1] // 2
    return jnp.concatenate([w[..., half:], w[..., :half]], axis=-1)


def _mla_weights(w_in, w_q_b, w_kv_b, q_nope_norm, k_nope_norm, q_rope_norm, k_rope_norm):
    D = w_in.shape[0]
    pad = MLA_HEAD_PAD - QK_NOPE - QK_ROPE
    z = lambda *s: jnp.zeros(s, F32)
    base = 2 * CONV_DIM + Q_LORA + KV_LORA
    w_kr = w_in[:, base:base + QK_ROPE]
    grp = lambda w: jnp.concatenate([z(D, QK_NOPE), w, z(D, pad)], axis=1)
    w_in_p = jnp.concatenate([w_in[:, :base], grp(w_kr), grp(_swap_halves(w_kr))], axis=1)

    wq = w_q_b.reshape(Q_LORA, MLA_HEADS, QK_NOPE + QK_ROPE)
    wq_nope, wq_rope = wq[..., :QK_NOPE], wq[..., QK_NOPE:]
    zq = z(Q_LORA, MLA_HEADS, pad)
    wq_p = jnp.concatenate([wq_nope, wq_rope, zq], axis=-1).reshape(Q_LORA, -1)
    wq_sw = jnp.concatenate([jnp.zeros_like(wq_nope), _swap_halves(wq_rope), zq], axis=-1).reshape(Q_LORA, -1)
    wq_all = jnp.concatenate([wq_p, wq_sw], axis=1)

    wkv = w_kv_b.reshape(KV_LORA, MLA_HEADS, QK_NOPE + V_HEAD)
    wk_p = jnp.concatenate([wkv[..., :QK_NOPE], z(KV_LORA, MLA_HEADS, MLA_HEAD_PAD - QK_NOPE)],
                           axis=-1).reshape(KV_LORA, -1)
    wv_p = wkv[..., QK_NOPE:].reshape(KV_LORA, -1)

    lane = jnp.arange(MLA_HEAD_PAD)
    in_nope = lane < QK_NOPE
    in_rope = (lane >= QK_NOPE) & (lane < QK_NOPE + QK_ROPE)
    seg = (jnp.where(in_nope[:, None] & in_nope[None, :], 1.0 / QK_NOPE, 0.0)
           + jnp.where(in_rope[:, None] & in_rope[None, :], 1.0 / QK_ROPE, 0.0))
    vec = lambda nope, rope: jnp.concatenate([nope, rope, z(pad)]).reshape(1, MLA_HEAD_PAD)
    gq = vec(q_nope_norm, q_rope_norm)
    gqs = vec(jnp.zeros_like(q_nope_norm), _swap_halves(q_rope_norm))
    gk = vec(k_nope_norm, z(QK_ROPE))
    gkr = vec(z(QK_NOPE), k_rope_norm)
    gkrs = vec(z(QK_NOPE), _swap_halves(k_rope_norm))
    return (w_in_p.astype(BF16), wq_all.astype(BF16), wk_p.astype(BF16), wv_p.astype(BF16),
            seg.astype(BF16), gq, gqs, gk, gkr, gkrs)


def kernel(x, mem, positions, ffn1_norm, ffn1_w_gate, ffn1_w_up, ffn1_w_down, ffn2_norm, ffn2_w_gate, ffn2_w_up, ffn2_w_down, mix_norm, xattn_norm, mem_norm, xattn_wq, xattn_wk, xattn_wv, xattn_wo, xattn_q_norm, xattn_k_norm, ev_w_in, ev_conv_w, ev_conv_b, ev_conv_ln_g, ev_conv_ln_b, ev_q_a_norm, ev_w_q_b, ev_kv_a_norm, ev_w_kv_b, ev_q_nope_norm, ev_k_nope_norm, ev_q_rope_norm, ev_k_rope_norm, ev_w_out, od_w_in, od_gn_g, od_gn_b, od_w_out):
    B, S, D = x.shape
    depth = ffn1_norm.shape[0]
    tm = min(TOKEN_TILE, S)
    assert S % tm == 0 and tm % RET_CHUNK == 0 and tm % CONV_HALO == 0
    bf = lambda w: w.astype(BF16)

    h = x.reshape(B * S, D)
    cos_r, sin_r, cm, sm = _rope_tables(positions, tm)
    decay, xi, zeta, gch = _retention_constants()

    for l in range(depth):
        h = _ffn(h, ffn1_norm[l], bf(ffn1_w_gate[l]), bf(ffn1_w_up[l]), bf(ffn1_w_down[l]), tm)
        if l % 2 == 0:
            e = l // 2
            (w_in_p, wq_p, wk_p, wv_p, seg, gq, gqs, gk, gkr, gkrs) = _mla_weights(
                ev_w_in[e], ev_w_q_b[e], ev_w_kv_b[e], ev_q_nope_norm[e], ev_k_nope_norm[e],
                ev_q_rope_norm[e], ev_k_rope_norm[e])
            a, q, k, v = _even_pre(h, mix_norm[l], w_in_p, cm, sm, ev_q_a_norm[e].reshape(1, -1), wq_p,
                                   ev_kv_a_norm[e].reshape(1, -1), wk_p, wv_p, seg, gq, gqs, gk, gkr,
                                   gkrs, tm)
            m = _mla_attention(q, k, v, B, S, tm)
            h = _even_post(h, a, m, ev_conv_w[e], ev_conv_b[e], ev_conv_ln_g[e], ev_conv_ln_b[e],
                           bf(ev_w_out[e][:CONV_DIM]), bf(ev_w_out[e][CONV_DIM:]), tm, S)
        else:
            o = l // 2
            q, k, v, gate = _odd_pre(h, mix_norm[l], bf(od_w_in[o]), cos_r, sin_r, tm)
            h = _retention(h, q, k, v, gate, decay, xi, zeta, gch, od_gn_g[o], od_gn_b[o],
                           bf(od_w_out[o]), B, S, tm)
        kt, vm = _memkv(mem, mem_norm[l], bf(xattn_wk[l]), bf(xattn_wv[l]), xattn_k_norm[l])
        h = _xattn(h, xattn_norm[l], bf(xattn_wq[l]), xattn_q_norm[l], kt, vm, bf(xattn_wo[l]), tm, S)
        h = _ffn(h, ffn2_norm[l], bf(ffn2_w_gate[l]), bf(ffn2_w_up[l]), bf(ffn2_w_down[l]), tm)
    return h.reshape(B, S, D)
```

```python
import functools
import math

import jax
import jax.numpy as jnp
from jax import lax
from jax.experimental import pallas as pl
from jax.experimental.pallas import tpu as pltpu

F32 = jnp.float32
BF16 = jnp.bfloat16

EPS = 1e-6
ROPE_THETA = 10000.0

D_FF = 2816
CONV_DIM = 512
CONV_WIDTH = 31
MLA_HEADS = 8
Q_LORA = 256
KV_LORA = 128
QK_NOPE = 64
QK_ROPE = 32
V_HEAD = 64
RET_HEADS = 4
RET_QK = 256
RET_V = 512
X_HEADS = 4

V7X_LANES = 128
V7X_MXU_DIM = 256
V7X_VMEM_LIMIT_BYTES = 56 * 1024 * 1024

TOKEN_TILE = 512
RET_BLOCK = 256
CONV_HALO = 32
MLA_HEAD_PAD = 128


def _params(*sem):
    return pltpu.CompilerParams(dimension_semantics=sem, vmem_limit_bytes=V7X_VMEM_LIMIT_BYTES)


def _const_spec(shape):
    nd = len(shape)
    return pl.BlockSpec(shape, lambda *_: (0,) * nd, pipeline_mode=pl.Buffered(1))


def _rms(x, gain):
    return x * lax.rsqrt(jnp.mean(x * x, axis=-1, keepdims=True) + EPS) * gain


def _silu(x):
    return x / (1.0 + jnp.exp(-x))


def _dot(a, b):
    return jnp.dot(a, b, preferred_element_type=F32)


def _dot_nt(a, b):
    return lax.dot_general(a, b, (((1,), (1,)), ((), ())), preferred_element_type=F32)


def _dot_tn(a, b):
    return lax.dot_general(a, b, (((0,), (0,)), ((), ())), preferred_element_type=F32)


def _rope_kernel(pos_ref, inv_r_ref, inv_m_ref, sgn_m_ref, cos_r_ref, sin_r_ref, cm_ref, sm_ref):
    pos = pos_ref[...].astype(F32)
    ang_r = pos * inv_r_ref[...]
    cos_r_ref[...] = jnp.cos(ang_r)
    sin_r_ref[...] = jnp.sin(ang_r)
    ang_m = pos * inv_m_ref[...]
    cm_ref[...] = jnp.cos(ang_m)
    sm_ref[...] = jnp.sin(ang_m) * sgn_m_ref[...]


def _rope_tables(positions, tm):
    T = positions.size
    pos = positions.reshape(T, 1)
    inv_r = (ROPE_THETA ** (-jnp.arange(0, RET_QK, 2, dtype=F32) / RET_QK)).reshape(1, RET_QK // 2)
    inv_m16 = ROPE_THETA ** (-jnp.arange(0, QK_ROPE, 2, dtype=F32) / QK_ROPE)
    half = QK_ROPE // 2
    pad = MLA_HEAD_PAD - QK_NOPE - QK_ROPE
    inv_m = jnp.concatenate([jnp.zeros((QK_NOPE,), F32), inv_m16, inv_m16, jnp.zeros((pad,), F32)])
    sgn_m = jnp.concatenate([jnp.zeros((QK_NOPE,), F32), -jnp.ones((half,), F32),
                             jnp.ones((half,), F32), jnp.zeros((pad,), F32)])
    row = pl.BlockSpec((tm, V7X_LANES), lambda i: (i, 0))
    vec = _const_spec((1, V7X_LANES))
    out = jax.ShapeDtypeStruct((T, V7X_LANES), F32)
    return pl.pallas_call(
        _rope_kernel,
        grid=(T // tm,),
        in_specs=[pl.BlockSpec((tm, 1), lambda i: (i, 0)), vec, vec, vec],
        out_specs=[row, row, row, row],
        out_shape=[out, out, out, out],
        compiler_params=_params("parallel"),
        name="rope_tables",
    )(pos, inv_r, inv_m.reshape(1, -1), sgn_m.reshape(1, -1))


def _ff_chunks(d_ff, width):
    edges = list(range(0, d_ff, width)) + [d_ff]
    return [(a, b) for a, b in zip(edges[:-1], edges[1:])]


def _ffn_kernel(x_ref, g_ref, wg_ref, wu_ref, wd_ref, o_ref, h_ref):
    x = x_ref[...]
    xn = _rms(x, g_ref[...]).astype(BF16)
    for a, b in _ff_chunks(h_ref.shape[1], 2 * V7X_MXU_DIM):
        gate = _dot(xn, wg_ref[:, a:b])
        up = _dot(xn, wu_ref[:, a:b])
        h_ref[:, a:b] = (_silu(gate) * up).astype(BF16)
    o_ref[...] = x + 0.5 * _dot(h_ref[...], wd_ref[...])


def _ffn(x, gain, w_gate, w_up, w_down, tm):
    T, D = x.shape
    d_ff = w_gate.shape[1]
    row = pl.BlockSpec((tm, D), lambda i: (i, 0))
    return pl.pallas_call(
        _ffn_kernel,
        grid=(T // tm,),
        in_specs=[row, _const_spec((1, D)), _const_spec((D, d_ff)), _const_spec((D, d_ff)),
                  _const_spec((d_ff, D))],
        out_specs=row,
        out_shape=jax.ShapeDtypeStruct((T, D), F32),
        scratch_shapes=[pltpu.VMEM((tm, d_ff), BF16)],
        compiler_params=_params("parallel"),
        name="ffn",
    )(x, gain.reshape(1, D), w_gate, w_up, w_down)


def _memkv_kernel(mem_ref, g_ref, wk_ref, wv_ref, kn_ref, kt_ref, v_ref):
    mn = _rms(mem_ref[...], g_ref[...]).astype(BF16)
    k = _dot(mn, wk_ref[...])
    hd = kn_ref.shape[1]
    for h in range(X_HEADS):
        kh = _rms(k[:, h * hd:(h + 1) * hd], kn_ref[...])
        kt_ref[0, h * hd:(h + 1) * hd, :] = kh.T.astype(BF16)
    v_ref[0] = _dot(mn, wv_ref[...]).astype(BF16)


def _memkv(mem, gain, wk, wv, k_norm):
    B, M, D = mem.shape
    hd = D // X_HEADS
    return pl.pallas_call(
        _memkv_kernel,
        grid=(B,),
        in_specs=[pl.BlockSpec((M, D), lambda b: (b, 0)), _const_spec((1, D)),
                  _const_spec((D, D)), _const_spec((D, D)), _const_spec((1, hd))],
        out_specs=[pl.BlockSpec((1, D, M), lambda b: (b, 0, 0)),
                   pl.BlockSpec((1, M, D), lambda b: (b, 0, 0))],
        out_shape=[jax.ShapeDtypeStruct((B, D, M), BF16), jax.ShapeDtypeStruct((B, M, D), BF16)],
        compiler_params=_params("parallel"),
        name="xattn_memkv",
    )(mem.reshape(B * M, D), gain.reshape(1, D), wk, wv, k_norm.reshape(1, hd))


def _xattn_kernel(x_ref, g_ref, wq_ref, qn_ref, kt_ref, v_ref, wo_ref, o_ref, att_ref):
    x = x_ref[...]
    xn = _rms(x, g_ref[...]).astype(BF16)
    q = _dot(xn, wq_ref[...])
    hd = qn_ref.shape[1]
    scale = hd ** -0.5
    for h in range(X_HEADS):
        sl = slice(h * hd, (h + 1) * hd)
        qh = (_rms(q[:, sl], qn_ref[...]) * scale).astype(BF16)
        s = _dot(qh, kt_ref[0, sl, :])
        p = jnp.exp(s - jnp.max(s, axis=-1, keepdims=True))
        l = jnp.sum(p, axis=-1, keepdims=True)
        att_ref[:, sl] = (_dot(p.astype(BF16), v_ref[0, :, sl]) / l).astype(BF16)
    o_ref[...] = x + _dot(att_ref[...], wo_ref[...])


def _xattn(x, gain, wq, q_norm, kt, v, wo, tm, seq):
    T, D = x.shape
    M = v.shape[1]
    hd = D // X_HEADS
    per_seq = seq // tm
    row = pl.BlockSpec((tm, D), lambda i: (i, 0))
    return pl.pallas_call(
        _xattn_kernel,
        grid=(T // tm,),
        in_specs=[row, _const_spec((1, D)), _const_spec((D, D)), _const_spec((1, hd)),
                  pl.BlockSpec((1, D, M), lambda i: (i // per_seq, 0, 0)),
                  pl.BlockSpec((1, M, D), lambda i: (i // per_seq, 0, 0)),
                  _const_spec((D, D))],
        out_specs=row,
        out_shape=jax.ShapeDtypeStruct((T, D), F32),
        scratch_shapes=[pltpu.VMEM((tm, D), BF16)],
        compiler_params=_params("parallel"),
        name="xattn",
    )(x, gain.reshape(1, D), wq, q_norm.reshape(1, hd), kt, v, wo)


def _even_pre_kernel(x_ref, g_ref, win_ref, cm_ref, sm_ref, qa_ref, wq_ref, kva_ref, wk_ref, wv_ref,
                     seg_ref, gq_ref, gqs_ref, gk_ref, gkr_ref, gkrs_ref,
                     a_ref, qt_ref, k_ref, vt_ref):
    xn = _rms(x_ref[...], g_ref[...]).astype(BF16)
    c = CONV_DIM
    a_ref[...] = _dot(xn, win_ref[:, 0:c]) / (1.0 + jnp.exp(-_dot(xn, win_ref[:, c:2 * c])))
    o = 2 * c
    zq = _dot(xn, win_ref[:, o:o + Q_LORA])
    o += Q_LORA
    zkv = _dot(xn, win_ref[:, o:o + KV_LORA])
    o += KV_LORA
    zkr = _dot(xn, win_ref[:, o:o + MLA_HEAD_PAD])
    o += MLA_HEAD_PAD
    zkr_sw = _dot(xn, win_ref[:, o:o + MLA_HEAD_PAD])
    cm = cm_ref[...]
    sm = sm_ref[...]

    rs_kr = lax.rsqrt(jnp.sum(zkr * zkr, axis=-1, keepdims=True) * (1.0 / QK_ROPE) + EPS)
    k_rope = (zkr * rs_kr * gkr_ref[...]) * cm + (zkr_sw * rs_kr * gkrs_ref[...]) * sm

    zq_n = _rms(zq, qa_ref[...]).astype(BF16)
    zkv_n = _rms(zkv, kva_ref[...]).astype(BF16)
    vt_ref[0] = _dot(zkv_n, wv_ref[...]).T.astype(BF16)
    hp = MLA_HEAD_PAD
    nq = MLA_HEADS * hp
    seg = seg_ref[...]
    for h in range(MLA_HEADS):
        sl = slice(h * hp, (h + 1) * hp)
        qh = _dot(zq_n, wq_ref[:, sl])
        qh_sw = _dot(zq_n, wq_ref[:, nq + h * hp:nq + (h + 1) * hp])
        rs = lax.rsqrt(_dot((qh * qh).astype(BF16), seg) + EPS)
        qr = (qh * rs * gq_ref[...]) * cm + (qh_sw * rs * gqs_ref[...]) * sm
        qt_ref[0, sl, :] = (qr * MLA_Q_SCALE).T.astype(BF16)
        kh = _dot(zkv_n, wk_ref[:, sl])
        rk = lax.rsqrt(_dot((kh * kh).astype(BF16), seg) + EPS)
        k_ref[:, sl] = (kh * rk * gk_ref[...] + k_rope).astype(BF16)


def _even_pre(x, gain, w_in_p, cm, sm, qa, wq_p, kva, wk_p, wv_p, seg, gq, gqs, gk, gkr, gkrs, tm):
    T, D = x.shape
    row = lambda w: pl.BlockSpec((tm, w), lambda i: (i, 0))
    hq, hv = MLA_HEADS * MLA_HEAD_PAD, MLA_HEADS * V_HEAD
    consts = [gain.reshape(1, D), w_in_p]
    tables = [cm, sm]
    rest = [qa, wq_p, kva, wk_p, wv_p, seg, gq, gqs, gk, gkr, gkrs]
    return pl.pallas_call(
        _even_pre_kernel,
        grid=(T // tm,),
        in_specs=[row(D)] + [_const_spec(a.shape) for a in consts] + [row(V7X_LANES)] * 2
                 + [_const_spec(a.shape) for a in rest],
        out_specs=[row(CONV_DIM), pl.BlockSpec((1, hq, tm), lambda i: (i, 0, 0)), row(hq),
                   pl.BlockSpec((1, hv, tm), lambda i: (i, 0, 0))],
        out_shape=[jax.ShapeDtypeStruct((T, CONV_DIM), F32), jax.ShapeDtypeStruct((T // tm, hq, tm), BF16),
                   jax.ShapeDtypeStruct((T, hq), BF16), jax.ShapeDtypeStruct((T // tm, hv, tm), BF16)],
        compiler_params=_params("parallel"),
        name="even_pre",
    )(x, *consts, *tables, *rest)


MLA_HEADS_PER_STEP = 4
MLA_Q_SCALE = (QK_NOPE + QK_ROPE) ** -0.5 * math.log2(math.e)


MLA_SUM_ROWS = 16


def _mla_kernel(qt_ref, k_ref, vt_ref, o_ref, m_ref, acc_ref, s_ref):
    qi = pl.program_id(2)
    tq = qt_ref.shape[2]
    hp = MLA_HEAD_PAD
    ones = jnp.ones((MLA_SUM_ROWS, tq), BF16)

    m_ref[...] = jnp.full(m_ref.shape, -jnp.inf, F32)
    acc_ref[...] = jnp.zeros(acc_ref.shape, F32)

    def block(j, masked):
        rows = pl.ds(pl.multiple_of(j * tq, tq), tq)
        for h in range(MLA_HEADS_PER_STEP):
            s_ref[h] = _dot(k_ref[rows, h * hp:(h + 1) * hp], qt_ref[0, h * hp:(h + 1) * hp, :])
        for h in range(MLA_HEADS_PER_STEP):
            s = s_ref[h]
            if masked:
                key = lax.broadcasted_iota(jnp.int32, s.shape, 0)
                qry = lax.broadcasted_iota(jnp.int32, s.shape, 1)
                s = jnp.where(key <= qry, s, -jnp.inf)
            m_old = m_ref[h]
            m_new = jnp.maximum(m_old, jnp.max(s, axis=0, keepdims=True))
            p = jnp.exp2(s - m_new).astype(BF16)
            v1 = jnp.concatenate([vt_ref[j, h * V_HEAD:(h + 1) * V_HEAD, :], ones], axis=0)
            acc_ref[h] = jnp.exp2(m_old - m_new) * acc_ref[h] + _dot(v1, p)
            m_ref[h] = m_new

    def body(j, carry):
        block(j, False)
        return carry

    lax.fori_loop(0, qi, body, 0)
    block(qi, True)
    out = [acc_ref[h, 0:V_HEAD, :] / acc_ref[h, V_HEAD:V_HEAD + 1, :] for h in range(MLA_HEADS_PER_STEP)]
    o_ref[...] = jnp.concatenate(out, axis=0).T.astype(BF16)


def _mla_attention(qt, k, vt, batch, seq, tq):
    T = k.shape[0]
    hs = MLA_HEADS_PER_STEP
    nq = seq // tq
    qw, vw = hs * MLA_HEAD_PAD, hs * V_HEAD
    return pl.pallas_call(
        _mla_kernel,
        grid=(batch, MLA_HEADS // hs, nq),
        in_specs=[pl.BlockSpec((1, qw, tq), lambda b, h, i: (b * nq + i, h, 0)),
                  pl.BlockSpec((seq, qw), lambda b, h, i: (b, h)),
                  pl.BlockSpec((nq, vw, tq), lambda b, h, i: (b, h, 0))],
        out_specs=pl.BlockSpec((tq, vw), lambda b, h, i: (b * nq + i, h)),
        out_shape=jax.ShapeDtypeStruct((T, MLA_HEADS * V_HEAD), BF16),
        scratch_shapes=[pltpu.VMEM((hs, 1, tq), F32), pltpu.VMEM((hs, V_HEAD + MLA_SUM_ROWS, tq), F32),
                        pltpu.VMEM((hs, tq, tq), F32)],
        compiler_params=_params("parallel", "parallel", "arbitrary"),
        name="mla_attention",
    )(qt, k, vt)


CONV_ROWS = 128
V7X_SUBLANES = 8


def _causal_conv(win_ref, cw_ref, ts):
    off = CONV_HALO - (CONV_WIDTH - 1)
    sub = V7X_SUBLANES
    lane_blocks = []
    for lb in range(win_ref.shape[1] // V7X_LANES):
        lanes = slice(lb * V7X_LANES, (lb + 1) * V7X_LANES)
        row_blocks = []
        for base in range(0, ts, CONV_ROWS):
            y = None
            for r in range(sub):
                n = CONV_ROWS + (sub if r else 0)
                z = None
                for q in range(-(-(off + CONV_WIDTH) // sub)):
                    j = sub * q + r - off
                    if 0 <= j < CONV_WIDTH:
                        lo = base + sub * q
                        term = cw_ref[j:j + 1, lanes] * win_ref[lo:lo + n, lanes]
                        z = term if z is None else z + term
                z = z[r:r + CONV_ROWS]
                y = z if y is None else y + z
            row_blocks.append(y)
        lane_blocks.append(jnp.concatenate(row_blocks, axis=0))
    return jnp.concatenate(lane_blocks, axis=1)


def _even_post_kernel(x_ref, halo_ref, a_ref, m_ref, cw_ref, cb_ref, lg_ref, lb_ref, wa_ref, wm_ref,
                      o_ref, win_ref, *, per_seq):
    ts = a_ref.shape[0]
    first = (pl.program_id(0) % per_seq) == 0
    win_ref[0:CONV_HALO, :] = jnp.where(first, 0.0, halo_ref[...])
    win_ref[CONV_HALO:, :] = a_ref[...]
    y = _causal_conv(win_ref, cw_ref, ts) + cb_ref[...]
    mu = jnp.mean(y, axis=-1, keepdims=True)
    yc = y - mu
    var = jnp.mean(yc * yc, axis=-1, keepdims=True)
    act = _silu(yc * lax.rsqrt(var + EPS) * lg_ref[...] + lb_ref[...]).astype(BF16)
    o_ref[...] = x_ref[...] + _dot(act, wa_ref[...]) + _dot(m_ref[...], wm_ref[...])


def _even_post(x, a, m, conv_w, conv_b, ln_g, ln_b, w_out_a, w_out_m, ts, seq):
    T, D = x.shape
    C = a.shape[1]
    per_seq = seq // ts
    halo_blocks = ts // CONV_HALO
    row = lambda w: pl.BlockSpec((ts, w), lambda i: (i, 0))
    return pl.pallas_call(
        functools.partial(_even_post_kernel, per_seq=per_seq),
        grid=(T // ts,),
        in_specs=[row(D),
                  pl.BlockSpec((CONV_HALO, C), lambda i: (jnp.maximum(i * halo_blocks - 1, 0), 0)),
                  row(C), row(C), _const_spec((CONV_WIDTH, C)), _const_spec((1, C)),
                  _const_spec((1, C)), _const_spec((1, C)), _const_spec((C, D)), _const_spec((C, D))],
        out_specs=row(D),
        out_shape=jax.ShapeDtypeStruct((T, D), F32),
        scratch_shapes=[pltpu.VMEM((CONV_HALO + ts, C), F32)],
        compiler_params=_params("parallel"),
        name="even_post",
    )(x, a, a, m, conv_w, conv_b.reshape(1, C), ln_g.reshape(1, C), ln_b.reshape(1, C),
      w_out_a, w_out_m)


def _odd_pre_kernel(x_ref, g_ref, w_ref, cos_ref, sin_ref, q_ref, k_ref, v_ref, gate_ref):
    xn = _rms(x_ref[...], g_ref[...]).astype(BF16)
    cos = cos_ref[...]
    sin = sin_ref[...]
    half = RET_QK // 2
    nqk = RET_HEADS * RET_QK
    k_scale = RET_QK ** -0.5
    for h in range(RET_HEADS):
        for ref, base, scale in ((q_ref, 0, 1.0), (k_ref, nqk, k_scale)):
            lo = h * RET_QK
            z = _dot(xn, w_ref[:, base + lo:base + lo + RET_QK])
            x1, x2 = z[:, :half], z[:, half:]
            ref[:, lo:lo + half] = ((x1 * cos - x2 * sin) * scale).astype(BF16)
            ref[:, lo + half:lo + RET_QK] = ((x1 * sin + x2 * cos) * scale).astype(BF16)
    nv = RET_HEADS * RET_V
    for h in range(RET_HEADS):
        sl = slice(h * RET_V, (h + 1) * RET_V)
        v_ref[:, sl] = _dot(xn, w_ref[:, 2 * nqk + h * RET_V:2 * nqk + (h + 1) * RET_V]).astype(BF16)
        gate_ref[:, sl] = _dot(xn, w_ref[:, 2 * nqk + nv + h * RET_V:2 * nqk + nv + (h + 1) * RET_V])


def _odd_pre(x, gain, w_in, cos_r, sin_r, tm):
    T, D = x.shape
    nqk, nv = RET_HEADS * RET_QK, RET_HEADS * RET_V
    row = lambda w: pl.BlockSpec((tm, w), lambda i: (i, 0))
    return pl.pallas_call(
        _odd_pre_kernel,
        grid=(T // tm,),
        in_specs=[row(D), _const_spec((1, D)), _const_spec(w_in.shape), row(V7X_LANES), row(V7X_LANES)],
        out_specs=[row(nqk), row(nqk), row(nv), row(nv)],
        out_shape=[jax.ShapeDtypeStruct((T, nqk), BF16), jax.ShapeDtypeStruct((T, nqk), BF16),
                   jax.ShapeDtypeStruct((T, nv), BF16), jax.ShapeDtypeStruct((T, nv), F32)],
        compiler_params=_params("parallel"),
        name="odd_pre",
    )(x, gain.reshape(1, D), w_in, cos_r, sin_r)


def _retention_kernel(x_ref, q_ref, k_ref, v_ref, gate_ref, decay_ref, xi_ref, zeta_ref, gch_ref,
                      gng_ref, gnb_ref, wo_ref, o_ref, state_ref, y_ref):
    @pl.when(pl.program_id(1) == 0)
    def _():
        state_ref[...] = jnp.zeros(state_ref.shape, F32)

    ts = x_ref.shape[0]
    C = RET_BLOCK
    for c in range(ts // C):
        rows = slice(c * C, (c + 1) * C)
        for h in range(RET_HEADS):
            qk = slice(h * RET_QK, (h + 1) * RET_QK)
            vv = slice(h * RET_V, (h + 1) * RET_V)
            qc = q_ref[rows, qk]
            kc = k_ref[rows, qk]
            vc = v_ref[rows, vv]
            state = state_ref[h]
            scores = (_dot_nt(qc, kc) * decay_ref[h]).astype(BF16)
            out = _dot(scores, vc) + _dot(qc, state.astype(BF16)) * xi_ref[h]
            kz = (kc.astype(F32) * zeta_ref[h]).astype(BF16)
            state_ref[h] = state * gch_ref[h] + _dot_tn(kz, vc)
            mu = jnp.mean(out, axis=-1, keepdims=True)
            oc = out - mu
            var = jnp.mean(oc * oc, axis=-1, keepdims=True)
            gn = oc * lax.rsqrt(var + EPS) * gng_ref[:, vv] + gnb_ref[:, vv]
            y_ref[rows, vv] = (_silu(gate_ref[rows, vv]) * gn).astype(BF16)
    o_ref[...] = x_ref[...] + _dot(y_ref[...], wo_ref[...])


def _retention(x, q, k, v, gate, decay, xi, zeta, gch, gn_g, gn_b, w_out, batch, seq, ts):
    T, D = x.shape
    nqk, nv = RET_HEADS * RET_QK, RET_HEADS * RET_V
    per_seq = seq // ts
    row = lambda w: pl.BlockSpec((ts, w), lambda b, i: (b * per_seq + i, 0))
    return pl.pallas_call(
        _retention_kernel,
        grid=(batch, per_seq),
        in_specs=[row(D), row(nqk), row(nqk), row(nv), row(nv),
                  _const_spec(decay.shape), _const_spec(xi.shape), _const_spec(zeta.shape),
                  _const_spec(gch.shape), _const_spec((1, nv)), _const_spec((1, nv)),
                  _const_spec((nv, D))],
        out_specs=row(D),
        out_shape=jax.ShapeDtypeStruct((T, D), F32),
        scratch_shapes=[pltpu.VMEM((RET_HEADS, RET_QK, RET_V), F32), pltpu.VMEM((ts, nv), BF16)],
        compiler_params=_params("parallel", "arbitrary"),
        name="retention",
    )(x, q, k, v, gate, decay, xi, zeta, gch, gn_g.reshape(1, nv), gn_b.reshape(1, nv), w_out)


def _retention_constants():
    H, C = RET_HEADS, RET_BLOCK
    log_g = jnp.log1p(-jnp.exp2(-5.0 - jnp.arange(H, dtype=F32)))
    idx = jnp.arange(C, dtype=F32)
    diff = idx[:, None] - idx[None, :]
    decay = jnp.where(diff >= 0, jnp.exp(log_g[:, None, None] * jnp.maximum(diff, 0.0)), 0.0)
    xi = jnp.exp(log_g[:, None] * (idx + 1.0))[:, :, None]
    zeta = jnp.exp(log_g[:, None] * (C - 1.0 - idx))[:, :, None]
    gch = jnp.exp(log_g * C)[:, None, None]
    return decay, xi, zeta, gch


def _swap_halves(w):
    half = w.shape[-1] // 2
    return jnp.concatenate([w[..., half:], w[..., :half]], axis=-1)


def _mla_weights(w_in, w_q_b, w_kv_b, q_nope_norm, k_nope_norm, q_rope_norm, k_rope_norm):
    D = w_in.shape[0]
    pad = MLA_HEAD_PAD - QK_NOPE - QK_ROPE
    z = lambda *s: jnp.zeros(s, F32)
    base = 2 * CONV_DIM + Q_LORA + KV_LORA
    w_kr = w_in[:, base:base + QK_ROPE]
    grp = lambda w: jnp.concatenate([z(D, QK_NOPE), w, z(D, pad)], axis=1)
    w_in_p = jnp.concatenate([w_in[:, :base], grp(w_kr), grp(_swap_halves(w_kr))], axis=1)

    wq = w_q_b.reshape(Q_LORA, MLA_HEADS, QK_NOPE + QK_ROPE)
    wq_nope, wq_rope = wq[..., :QK_NOPE], wq[..., QK_NOPE:]
    zq = z(Q_LORA, MLA_HEADS, pad)
    wq_p = jnp.concatenate([wq_nope, wq_rope, zq], axis=-1).reshape(Q_LORA, -1)
    wq_sw = jnp.concatenate([jnp.zeros_like(wq_nope), _swap_halves(wq_rope), zq], axis=-1).reshape(Q_LORA, -1)
    wq_all = jnp.concatenate([wq_p, wq_sw], axis=1)

    wkv = w_kv_b.reshape(KV_LORA, MLA_HEADS, QK_NOPE + V_HEAD)
    wk_p = jnp.concatenate([wkv[..., :QK_NOPE], z(KV_LORA, MLA_HEADS, MLA_HEAD_PAD - QK_NOPE)],
                           axis=-1).reshape(KV_LORA, -1)
    wv_p = wkv[..., QK_NOPE:].reshape(KV_LORA, -1)

    lane = jnp.arange(MLA_HEAD_PAD)
    in_nope = lane < QK_NOPE
    in_rope = (lane >= QK_NOPE) & (lane < QK_NOPE + QK_ROPE)
    seg = (jnp.where(in_nope[:, None] & in_nope[None, :], 1.0 / QK_NOPE, 0.0)
           + jnp.where(in_rope[:, None] & in_rope[None, :], 1.0 / QK_ROPE, 0.0))
    vec = lambda nope, rope: jnp.concatenate([nope, rope, z(pad)]).reshape(1, MLA_HEAD_PAD)
    gq = vec(q_nope_norm, q_rope_norm)
    gqs = vec(jnp.zeros_like(q_nope_norm), _swap_halves(q_rope_norm))
    gk = vec(k_nope_norm, z(QK_ROPE))
    gkr = vec(z(QK_NOPE), k_rope_norm)
    gkrs = vec(z(QK_NOPE), _swap_halves(k_rope_norm))
    return (w_in_p.astype(BF16), wq_all.astype(BF16), wk_p.astype(BF16), wv_p.astype(BF16),
            seg.astype(BF16), gq, gqs, gk, gkr, gkrs)


def kernel(x, mem, positions, ffn1_norm, ffn1_w_gate, ffn1_w_up, ffn1_w_down, ffn2_norm, ffn2_w_gate, ffn2_w_up, ffn2_w_down, mix_norm, xattn_norm, mem_norm, xattn_wq, xattn_wk, xattn_wv, xattn_wo, xattn_q_norm, xattn_k_norm, ev_w_in, ev_conv_w, ev_conv_b, ev_conv_ln_g, ev_conv_ln_b, ev_q_a_norm, ev_w_q_b, ev_kv_a_norm, ev_w_kv_b, ev_q_nope_norm, ev_k_nope_norm, ev_q_rope_norm, ev_k_rope_norm, ev_w_out, od_w_in, od_gn_g, od_gn_b, od_w_out):
    B, S, D = x.shape
    depth = ffn1_norm.shape[0]
    tm = min(TOKEN_TILE, S)
    assert S % tm == 0 and tm % RET_BLOCK == 0 and tm % CONV_HALO == 0
    bf = lambda w: w.astype(BF16)

    h = x.reshape(B * S, D)
    cos_r, sin_r, cm, sm = _rope_tables(positions, tm)
    decay, xi, zeta, gch = _retention_constants()

    for l in range(depth):
        h = _ffn(h, ffn1_norm[l], bf(ffn1_w_gate[l]), bf(ffn1_w_up[l]), bf(ffn1_w_down[l]), tm)
        if l % 2 == 0:
            e = l // 2
            (w_in_p, wq_p, wk_p, wv_p, seg, gq, gqs, gk, gkr, gkrs) = _mla_weights(
                ev_w_in[e], ev_w_q_b[e], ev_w_kv_b[e], ev_q_nope_norm[e], ev_k_nope_norm[e],
                ev_q_rope_norm[e], ev_k_rope_norm[e])
            a, q, k, v = _even_pre(h, mix_norm[l], w_in_p, cm, sm, ev_q_a_norm[e].reshape(1, -1), wq_p,
                                   ev_kv_a_norm[e].reshape(1, -1), wk_p, wv_p, seg, gq, gqs, gk, gkr,
                                   gkrs, tm)
            m = _mla_attention(q, k, v, B, S, tm)
            h = _even_post(h, a, m, ev_conv_w[e], ev_conv_b[e], ev_conv_ln_g[e], ev_conv_ln_b[e],
                           bf(ev_w_out[e][:CONV_DIM]), bf(ev_w_out[e][CONV_DIM:]), tm, S)
        else:
            o = l // 2
            q, k, v, gate = _odd_pre(h, mix_norm[l], bf(od_w_in[o]), cos_r, sin_r, tm)
            h = _retention(h, q, k, v, gate, decay, xi, zeta, gch, od_gn_g[o], od_gn_b[o],
                           bf(od_w_out[o]), B, S, tm)
        kt, vm = _memkv(mem, mem_norm[l], bf(xattn_wk[l]), bf(xattn_wv[l]), xattn_k_norm[l])
        h = _xattn(h, xattn_norm[l], bf(xattn_wq[l]), xattn_q_norm[l], kt, vm, bf(xattn_wo[l]), tm, S)
        h = _ffn(h, ffn2_norm[l], bf(ffn2_w_gate[l]), bf(ffn2_w_up[l]), bf(ffn2_w_down[l]), tm)
    return h.reshape(B, S, D)
```

```python
import functools
import math

import jax
import jax.numpy as jnp
from jax import lax
from jax.experimental import pallas as pl
from jax.experimental.pallas import tpu as pltpu

F32 = jnp.float32
BF16 = jnp.bfloat16

EPS = 1e-6
ROPE_THETA = 10000.0

D_FF = 2816
CONV_DIM = 512
CONV_WIDTH = 31
MLA_HEADS = 8
Q_LORA = 256
KV_LORA = 128
QK_NOPE = 64
QK_ROPE = 32
V_HEAD = 64
RET_HEADS = 4
RET_QK = 256
RET_V = 512
X_HEADS = 4

V7X_LANES = 128
V7X_MXU_DIM = 256
V7X_VMEM_LIMIT_BYTES = 56 * 1024 * 1024

TOKEN_TILE = 512
FFN_TOKEN_TILE = 1024
RET_BLOCK = 256
CONV_HALO = 32
MLA_HEAD_PAD = 128


def _params(*sem):
    return pltpu.CompilerParams(dimension_semantics=sem, vmem_limit_bytes=V7X_VMEM_LIMIT_BYTES)


def _const_spec(shape):
    nd = len(shape)
    return pl.BlockSpec(shape, lambda *_: (0,) * nd, pipeline_mode=pl.Buffered(1))


def _rms(x, gain):
    return x * lax.rsqrt(jnp.mean(x * x, axis=-1, keepdims=True) + EPS) * gain


def _silu(x):
    return x / (1.0 + jnp.exp(-x))


def _dot(a, b):
    return jnp.dot(a, b, preferred_element_type=F32)


def _dot_nt(a, b):
    return lax.dot_general(a, b, (((1,), (1,)), ((), ())), preferred_element_type=F32)


def _dot_tn(a, b):
    return lax.dot_general(a, b, (((0,), (0,)), ((), ())), preferred_element_type=F32)


def _rope_kernel(pos_ref, inv_ref, cos_ref, sin_ref):
    ang = pos_ref[...].astype(F32) * inv_ref[...]
    cos_ref[...] = jnp.cos(ang)
    sin_ref[...] = jnp.sin(ang)


def _rope_tables(positions, tm):
    T = positions.size
    inv = (ROPE_THETA ** (-jnp.arange(0, RET_QK, 2, dtype=F32) / RET_QK)).reshape(1, RET_QK // 2)
    row = pl.BlockSpec((tm, V7X_LANES), lambda i: (i, 0))
    out = jax.ShapeDtypeStruct((T, V7X_LANES), F32)
    cos_r, sin_r = pl.pallas_call(
        _rope_kernel,
        grid=(T // tm,),
        in_specs=[pl.BlockSpec((tm, 1), lambda i: (i, 0)), _const_spec((1, V7X_LANES))],
        out_specs=[row, row],
        out_shape=[out, out],
        compiler_params=_params("parallel"),
        name="rope_tables",
    )(positions.reshape(T, 1), inv)
    assert RET_QK % QK_ROPE == 0
    step = RET_QK // QK_ROPE
    cos_m, sin_m = cos_r[:, ::step], sin_r[:, ::step]
    pad = MLA_HEAD_PAD - QK_NOPE - QK_ROPE
    one, zero = jnp.ones((T, 1), F32), jnp.zeros((T, 1), F32)
    cm = jnp.concatenate([jnp.tile(one, (1, QK_NOPE)), cos_m, cos_m, jnp.tile(one, (1, pad))], axis=1)
    sm = jnp.concatenate([jnp.tile(zero, (1, QK_NOPE)), -sin_m, sin_m, jnp.tile(zero, (1, pad))], axis=1)
    return cos_r, sin_r, cm, sm


def _ff_chunks(d_ff, width):
    edges = list(range(0, d_ff, width)) + [d_ff]
    return [(a, b) for a, b in zip(edges[:-1], edges[1:])]


def _ffn_kernel(x_ref, g_ref, wg_ref, wu_ref, wd_ref, o_ref, h_ref):
    x = x_ref[...]
    xn = _rms(x, g_ref[...]).astype(BF16)
    for a, b in _ff_chunks(h_ref.shape[1], 2 * V7X_MXU_DIM):
        gate = _dot(xn, wg_ref[:, a:b])
        up = _dot(xn, wu_ref[:, a:b])
        h_ref[:, a:b] = (_silu(gate) * up).astype(BF16)
    o_ref[...] = x + 0.5 * _dot(h_ref[...], wd_ref[...])


def _ffn(x, gain, w_gate, w_up, w_down, tm):
    T, D = x.shape
    d_ff = w_gate.shape[1]
    row = pl.BlockSpec((tm, D), lambda i: (i, 0))
    return pl.pallas_call(
        _ffn_kernel,
        grid=(T // tm,),
        in_specs=[row, _const_spec((1, D)), _const_spec((D, d_ff)), _const_spec((D, d_ff)),
                  _const_spec((d_ff, D))],
        out_specs=row,
        out_shape=jax.ShapeDtypeStruct((T, D), F32),
        scratch_shapes=[pltpu.VMEM((tm, d_ff), BF16)],
        compiler_params=_params("parallel"),
        name="ffn",
    )(x, gain.reshape(1, D), w_gate, w_up, w_down)


def _memkv_kernel(mem_ref, g_ref, wk_ref, wv_ref, kn_ref, kt_ref, v_ref):
    mn = _rms(mem_ref[...], g_ref[...]).astype(BF16)
    k = _dot(mn, wk_ref[...])
    hd = kn_ref.shape[1]
    for h in range(X_HEADS):
        kh = _rms(k[:, h * hd:(h + 1) * hd], kn_ref[...])
        kt_ref[0, h * hd:(h + 1) * hd, :] = kh.T.astype(BF16)
    v_ref[0] = _dot(mn, wv_ref[...]).astype(BF16)


def _memkv(mem, gain, wk, wv, k_norm):
    B, M, D = mem.shape
    hd = D // X_HEADS
    return pl.pallas_call(
        _memkv_kernel,
        grid=(B,),
        in_specs=[pl.BlockSpec((M, D), lambda b: (b, 0)), _const_spec((1, D)),
                  _const_spec((D, D)), _const_spec((D, D)), _const_spec((1, hd))],
        out_specs=[pl.BlockSpec((1, D, M), lambda b: (b, 0, 0)),
                   pl.BlockSpec((1, M, D), lambda b: (b, 0, 0))],
        out_shape=[jax.ShapeDtypeStruct((B, D, M), BF16), jax.ShapeDtypeStruct((B, M, D), BF16)],
        compiler_params=_params("parallel"),
        name="xattn_memkv",
    )(mem.reshape(B * M, D), gain.reshape(1, D), wk, wv, k_norm.reshape(1, hd))


def _xattn_kernel(x_ref, g_ref, wq_ref, qn_ref, kt_ref, v_ref, wo_ref, o_ref, att_ref):
    x = x_ref[...]
    xn = _rms(x, g_ref[...]).astype(BF16)
    q = _dot(xn, wq_ref[...])
    hd = qn_ref.shape[1]
    scale = hd ** -0.5
    for h in range(X_HEADS):
        sl = slice(h * hd, (h + 1) * hd)
        qh = (_rms(q[:, sl], qn_ref[...]) * scale).astype(BF16)
        s = _dot(qh, kt_ref[0, sl, :])
        p = jnp.exp(s - jnp.max(s, axis=-1, keepdims=True))
        l = jnp.sum(p, axis=-1, keepdims=True)
        att_ref[:, sl] = (_dot(p.astype(BF16), v_ref[0, :, sl]) / l).astype(BF16)
    o_ref[...] = x + _dot(att_ref[...], wo_ref[...])


def _xattn(x, gain, wq, q_norm, kt, v, wo, tm, seq):
    T, D = x.shape
    M = v.shape[1]
    hd = D // X_HEADS
    per_seq = seq // tm
    row = pl.BlockSpec((tm, D), lambda i: (i, 0))
    return pl.pallas_call(
        _xattn_kernel,
        grid=(T // tm,),
        in_specs=[row, _const_spec((1, D)), _const_spec((D, D)), _const_spec((1, hd)),
                  pl.BlockSpec((1, D, M), lambda i: (i // per_seq, 0, 0)),
                  pl.BlockSpec((1, M, D), lambda i: (i // per_seq, 0, 0)),
                  _const_spec((D, D))],
        out_specs=row,
        out_shape=jax.ShapeDtypeStruct((T, D), F32),
        scratch_shapes=[pltpu.VMEM((tm, D), BF16)],
        compiler_params=_params("parallel"),
        name="xattn",
    )(x, gain.reshape(1, D), wq, q_norm.reshape(1, hd), kt, v, wo)


def _even_pre_kernel(x_ref, g_ref, win_ref, cm_ref, sm_ref, qa_ref, wq_ref, kva_ref, wk_ref, wv_ref,
                     seg_ref, gq_ref, gqs_ref, gk_ref, gkr_ref, gkrs_ref,
                     a_ref, qt_ref, k_ref, vt_ref):
    xn = _rms(x_ref[...], g_ref[...]).astype(BF16)
    c = CONV_DIM
    glu = _dot(xn, win_ref[:, 0:2 * c])
    a_ref[...] = glu[:, :c] / (1.0 + jnp.exp(-glu[:, c:]))
    hp = MLA_HEAD_PAD
    z = _dot(xn, win_ref[:, 2 * c:])
    zq = z[:, :Q_LORA]
    zkv = z[:, Q_LORA:Q_LORA + KV_LORA]
    zkr = z[:, Q_LORA + KV_LORA:Q_LORA + KV_LORA + hp]
    zkr_sw = z[:, Q_LORA + KV_LORA + hp:]
    cm = cm_ref[...]
    sm = sm_ref[...]

    rs_kr = lax.rsqrt(jnp.sum(zkr * zkr, axis=-1, keepdims=True) * (1.0 / QK_ROPE) + EPS)
    k_rope = (zkr * rs_kr * gkr_ref[...]) * cm + (zkr_sw * rs_kr * gkrs_ref[...]) * sm

    zq_n = _rms(zq, qa_ref[...]).astype(BF16)
    zkv_n = _rms(zkv, kva_ref[...]).astype(BF16)
    vt_ref[0] = _dot(zkv_n, wv_ref[...]).T.astype(BF16)
    pair = lambda t: jnp.concatenate([t, t], axis=1)
    cm2, sm2, k_rope2 = pair(cm), pair(sm), pair(k_rope)
    pw = 2 * hp
    nq = MLA_HEADS * hp
    seg = seg_ref[...]
    for h in range(MLA_HEADS // 2):
        sl = slice(h * pw, (h + 1) * pw)
        qh = _dot(zq_n, wq_ref[:, sl])
        qh_sw = _dot(zq_n, wq_ref[:, nq + h * pw:nq + (h + 1) * pw])
        rs = lax.rsqrt(_dot((qh * qh).astype(BF16), seg) + EPS)
        qr = (qh * rs * gq_ref[...]) * cm2 + (qh_sw * rs * gqs_ref[...]) * sm2
        qt_ref[0, sl, :] = (qr * MLA_Q_SCALE).T.astype(BF16)
        kh = _dot(zkv_n, wk_ref[:, sl])
        rk = lax.rsqrt(_dot((kh * kh).astype(BF16), seg) + EPS)
        k_ref[:, sl] = (kh * rk * gk_ref[...] + k_rope2).astype(BF16)


def _even_pre(x, gain, w_in_p, cm, sm, qa, wq_p, kva, wk_p, wv_p, seg, gq, gqs, gk, gkr, gkrs, tm):
    T, D = x.shape
    row = lambda w: pl.BlockSpec((tm, w), lambda i: (i, 0))
    hq, hv = MLA_HEADS * MLA_HEAD_PAD, MLA_HEADS * V_HEAD
    consts = [gain.reshape(1, D), w_in_p]
    tables = [cm, sm]
    rest = [qa, wq_p, kva, wk_p, wv_p, seg, gq, gqs, gk, gkr, gkrs]
    return pl.pallas_call(
        _even_pre_kernel,
        grid=(T // tm,),
        in_specs=[row(D)] + [_const_spec(a.shape) for a in consts] + [row(V7X_LANES)] * 2
                 + [_const_spec(a.shape) for a in rest],
        out_specs=[row(CONV_DIM), pl.BlockSpec((1, hq, tm), lambda i: (i, 0, 0)), row(hq),
                   pl.BlockSpec((1, hv, tm), lambda i: (i, 0, 0))],
        out_shape=[jax.ShapeDtypeStruct((T, CONV_DIM), F32), jax.ShapeDtypeStruct((T // tm, hq, tm), BF16),
                   jax.ShapeDtypeStruct((T, hq), BF16), jax.ShapeDtypeStruct((T // tm, hv, tm), BF16)],
        compiler_params=_params("parallel"),
        name="even_pre",
    )(x, *consts, *tables, *rest)


MLA_HEADS_PER_STEP = 8
MLA_Q_SCALE = (QK_NOPE + QK_ROPE) ** -0.5 * math.log2(math.e)


MLA_SUM_ROWS = 16


def _mla_kernel(qt_ref, k_ref, vt_ref, o_ref, m_ref, acc_ref, s_ref):
    qi = pl.program_id(2)
    tq = qt_ref.shape[2]
    hp = MLA_HEAD_PAD
    ones = jnp.ones((MLA_SUM_ROWS, tq), BF16)

    m_ref[...] = jnp.full(m_ref.shape, -jnp.inf, F32)
    acc_ref[...] = jnp.zeros(acc_ref.shape, F32)

    def block(j, masked):
        rows = pl.ds(pl.multiple_of(j * tq, tq), tq)
        for h in range(MLA_HEADS_PER_STEP):
            s_ref[h] = _dot(k_ref[rows, h * hp:(h + 1) * hp], qt_ref[0, h * hp:(h + 1) * hp, :])
        for h in range(MLA_HEADS_PER_STEP):
            s = s_ref[h]
            if masked:
                key = lax.broadcasted_iota(jnp.int32, s.shape, 0)
                qry = lax.broadcasted_iota(jnp.int32, s.shape, 1)
                s = jnp.where(key <= qry, s, -jnp.inf)
            m_old = m_ref[h]
            m_new = jnp.maximum(m_old, jnp.max(s, axis=0, keepdims=True))
            p = jnp.exp2(s - m_new).astype(BF16)
            v1 = jnp.concatenate([vt_ref[j, h * V_HEAD:(h + 1) * V_HEAD, :], ones], axis=0)
            acc_ref[h] = jnp.exp2(m_old - m_new) * acc_ref[h] + _dot(v1, p)
            m_ref[h] = m_new

    def body(j, carry):
        block(j, False)
        return carry

    lax.fori_loop(0, qi, body, 0)
    block(qi, True)
    out = [acc_ref[h, 0:V_HEAD, :] / acc_ref[h, V_HEAD:V_HEAD + 1, :] for h in range(MLA_HEADS_PER_STEP)]
    o_ref[...] = jnp.concatenate(out, axis=0).T.astype(BF16)


def _mla_attention(qt, k, vt, batch, seq, tq):
    T = k.shape[0]
    hs = MLA_HEADS_PER_STEP
    nq = seq // tq
    qw, vw = hs * MLA_HEAD_PAD, hs * V_HEAD
    return pl.pallas_call(
        _mla_kernel,
        grid=(batch, MLA_HEADS // hs, nq),
        in_specs=[pl.BlockSpec((1, qw, tq), lambda b, h, i: (b * nq + i, h, 0)),
                  pl.BlockSpec((seq, qw), lambda b, h, i: (b, h)),
                  pl.BlockSpec((nq, vw, tq), lambda b, h, i: (b, h, 0))],
        out_specs=pl.BlockSpec((tq, vw), lambda b, h, i: (b * nq + i, h)),
        out_shape=jax.ShapeDtypeStruct((T, MLA_HEADS * V_HEAD), BF16),
        scratch_shapes=[pltpu.VMEM((hs, 1, tq), F32), pltpu.VMEM((hs, V_HEAD + MLA_SUM_ROWS, tq), F32),
                        pltpu.VMEM((hs, tq, tq), F32)],
        compiler_params=_params("parallel", "parallel", "arbitrary"),
        name="mla_attention",
    )(qt, k, vt)


CONV_ROWS = 128
V7X_SUBLANES = 8


def _causal_conv(win_ref, cw_ref, ts):
    off = CONV_HALO - (CONV_WIDTH - 1)
    sub = V7X_SUBLANES
    lane_blocks = []
    for lb in range(win_ref.shape[1] // V7X_LANES):
        lanes = slice(lb * V7X_LANES, (lb + 1) * V7X_LANES)
        row_blocks = []
        for base in range(0, ts, CONV_ROWS):
            y = None
            for r in range(sub):
                n = CONV_ROWS + (sub if r else 0)
                z = None
                for q in range(-(-(off + CONV_WIDTH) // sub)):
                    j = sub * q + r - off
                    if 0 <= j < CONV_WIDTH:
                        lo = base + sub * q
                        term = cw_ref[j:j + 1, lanes] * win_ref[lo:lo + n, lanes]
                        z = term if z is None else z + term
                z = z[r:r + CONV_ROWS]
                y = z if y is None else y + z
            row_blocks.append(y)
        lane_blocks.append(jnp.concatenate(row_blocks, axis=0))
    return jnp.concatenate(lane_blocks, axis=1)


def _even_post_kernel(x_ref, halo_ref, a_ref, m_ref, cw_ref, cb_ref, lg_ref, lb_ref, wa_ref, wm_ref,
                      o_ref, win_ref, *, per_seq):
    ts = a_ref.shape[0]
    first = (pl.program_id(0) % per_seq) == 0
    win_ref[0:CONV_HALO, :] = jnp.where(first, 0.0, halo_ref[...])
    win_ref[CONV_HALO:, :] = a_ref[...]
    y = _causal_conv(win_ref, cw_ref, ts) + cb_ref[...]
    mu = jnp.mean(y, axis=-1, keepdims=True)
    yc = y - mu
    var = jnp.mean(yc * yc, axis=-1, keepdims=True)
    act = _silu(yc * lax.rsqrt(var + EPS) * lg_ref[...] + lb_ref[...]).astype(BF16)
    o_ref[...] = x_ref[...] + _dot(act, wa_ref[...]) + _dot(m_ref[...], wm_ref[...])


def _even_post(x, a, m, conv_w, conv_b, ln_g, ln_b, w_out_a, w_out_m, ts, seq):
    T, D = x.shape
    C = a.shape[1]
    per_seq = seq // ts
    halo_blocks = ts // CONV_HALO
    row = lambda w: pl.BlockSpec((ts, w), lambda i: (i, 0))
    return pl.pallas_call(
        functools.partial(_even_post_kernel, per_seq=per_seq),
        grid=(T // ts,),
        in_specs=[row(D),
                  pl.BlockSpec((CONV_HALO, C), lambda i: (jnp.maximum(i * halo_blocks - 1, 0), 0)),
                  row(C), row(C), _const_spec((CONV_WIDTH, C)), _const_spec((1, C)),
                  _const_spec((1, C)), _const_spec((1, C)), _const_spec((C, D)), _const_spec((C, D))],
        out_specs=row(D),
        out_shape=jax.ShapeDtypeStruct((T, D), F32),
        scratch_shapes=[pltpu.VMEM((CONV_HALO + ts, C), F32)],
        compiler_params=_params("parallel"),
        name="even_post",
    )(x, a, a, m, conv_w, conv_b.reshape(1, C), ln_g.reshape(1, C), ln_b.reshape(1, C),
      w_out_a, w_out_m)


def _odd_pre_kernel(x_ref, g_ref, w_ref, cos_ref, sin_ref, q_ref, kt_ref, v_ref, gate_ref):
    xn = _rms(x_ref[...], g_ref[...]).astype(BF16)
    cos = cos_ref[...]
    sin = sin_ref[...]
    half = RET_QK // 2
    nqk = RET_HEADS * RET_QK
    k_scale = RET_QK ** -0.5
    for h in range(RET_HEADS):
        lo = h * RET_QK
        z = _dot(xn, w_ref[:, lo:lo + RET_QK])
        x1, x2 = z[:, :half], z[:, half:]
        q_ref[:, lo:lo + half] = (x1 * cos - x2 * sin).astype(BF16)
        q_ref[:, lo + half:lo + RET_QK] = (x1 * sin + x2 * cos).astype(BF16)
        z = _dot(xn, w_ref[:, nqk + lo:nqk + lo + RET_QK])
        x1, x2 = z[:, :half], z[:, half:]
        k = jnp.concatenate([x1 * cos - x2 * sin, x1 * sin + x2 * cos], axis=1) * k_scale
        kt_ref[0, lo:lo + RET_QK, :] = k.T.astype(BF16)
    nv = RET_HEADS * RET_V
    for h in range(RET_HEADS):
        sl = slice(h * RET_V, (h + 1) * RET_V)
        v_ref[:, sl] = _dot(xn, w_ref[:, 2 * nqk + h * RET_V:2 * nqk + (h + 1) * RET_V]).astype(BF16)
        gate_ref[:, sl] = _dot(xn, w_ref[:, 2 * nqk + nv + h * RET_V:2 * nqk + nv + (h + 1) * RET_V])


def _odd_pre(x, gain, w_in, cos_r, sin_r, tm):
    T, D = x.shape
    nqk, nv = RET_HEADS * RET_QK, RET_HEADS * RET_V
    row = lambda w: pl.BlockSpec((tm, w), lambda i: (i, 0))
    return pl.pallas_call(
        _odd_pre_kernel,
        grid=(T // tm,),
        in_specs=[row(D), _const_spec((1, D)), _const_spec(w_in.shape), row(V7X_LANES), row(V7X_LANES)],
        out_specs=[row(nqk), pl.BlockSpec((1, nqk, tm), lambda i: (i, 0, 0)), row(nv), row(nv)],
        out_shape=[jax.ShapeDtypeStruct((T, nqk), BF16), jax.ShapeDtypeStruct((T // tm, nqk, tm), BF16),
                   jax.ShapeDtypeStruct((T, nv), BF16), jax.ShapeDtypeStruct((T, nv), F32)],
        compiler_params=_params("parallel"),
        name="odd_pre",
    )(x, gain.reshape(1, D), w_in, cos_r, sin_r)


def _retention_kernel(x_ref, q_ref, kt_ref, v_ref, gate_ref, decay_ref, xi_ref, zeta_ref, gch_ref,
                      gng_ref, gnb_ref, wo_ref, o_ref, state_ref, y_ref):
    @pl.when(pl.program_id(1) == 0)
    def _():
        state_ref[...] = jnp.zeros(state_ref.shape, F32)

    ts = x_ref.shape[0]
    C = RET_BLOCK
    for c in range(ts // C):
        rows = slice(c * C, (c + 1) * C)
        for h in range(RET_HEADS):
            qk = slice(h * RET_QK, (h + 1) * RET_QK)
            vv = slice(h * RET_V, (h + 1) * RET_V)
            qc = q_ref[rows, qk]
            kt = kt_ref[0, qk, rows]
            vc = v_ref[rows, vv]
            state = state_ref[h]
            scores = (_dot(qc, kt) * decay_ref[h]).astype(BF16)
            out = _dot(scores, vc) + _dot(qc, state.astype(BF16)) * xi_ref[h]
            kz = (kt.astype(F32) * zeta_ref[h]).astype(BF16)
            state_ref[h] = state * gch_ref[h] + _dot(kz, vc)
            mu = jnp.mean(out, axis=-1, keepdims=True)
            oc = out - mu
            var = jnp.mean(oc * oc, axis=-1, keepdims=True)
            gn = oc * lax.rsqrt(var + EPS) * gng_ref[:, vv] + gnb_ref[:, vv]
            y_ref[rows, vv] = (_silu(gate_ref[rows, vv]) * gn).astype(BF16)
    o_ref[...] = x_ref[...] + _dot(y_ref[...], wo_ref[...])


def _retention(x, q, k, v, gate, decay, xi, zeta, gch, gn_g, gn_b, w_out, batch, seq, ts):
    T, D = x.shape
    nqk, nv = RET_HEADS * RET_QK, RET_HEADS * RET_V
    per_seq = seq // ts
    row = lambda w: pl.BlockSpec((ts, w), lambda b, i: (b * per_seq + i, 0))
    return pl.pallas_call(
        _retention_kernel,
        grid=(batch, per_seq),
        in_specs=[row(D), row(nqk), pl.BlockSpec((1, nqk, ts), lambda b, i: (b * per_seq + i, 0, 0)),
                  row(nv), row(nv), _const_spec(decay.shape), _const_spec(xi.shape), _const_spec(zeta.shape),
                  _const_spec(gch.shape), _const_spec((1, nv)), _const_spec((1, nv)),
                  _const_spec((nv, D))],
        out_specs=row(D),
        out_shape=jax.ShapeDtypeStruct((T, D), F32),
        scratch_shapes=[pltpu.VMEM((RET_HEADS, RET_QK, RET_V), F32), pltpu.VMEM((ts, nv), BF16)],
        compiler_params=_params("parallel", "arbitrary"),
        name="retention",
    )(x, q, k, v, gate, decay, xi, zeta, gch, gn_g.reshape(1, nv), gn_b.reshape(1, nv), w_out)


def _retention_constants():
    H, C = RET_HEADS, RET_BLOCK
    log_g = jnp.log1p(-jnp.exp2(-5.0 - jnp.arange(H, dtype=F32)))
    idx = jnp.arange(C, dtype=F32)
    diff = idx[:, None] - idx[None, :]
    decay = jnp.where(diff >= 0, jnp.exp(log_g[:, None, None] * jnp.maximum(diff, 0.0)), 0.0)
    xi = jnp.exp(log_g[:, None] * (idx + 1.0))[:, :, None]
    zeta = jnp.exp(log_g[:, None] * (C - 1.0 - idx))[:, None, :]
    gch = jnp.exp(log_g * C)[:, None, None]
    return decay, xi, zeta, gch


def _swap_halves(w):
    half = w.shape[-1] // 2
    return jnp.concatenate([w[..., half:], w[..., :half]], axis=-1)


def _mla_weights(w_in, w_q_b, w_kv_b, q_nope_norm, k_nope_norm, q_rope_norm, k_rope_norm):
    D = w_in.shape[0]
    pad = MLA_HEAD_PAD - QK_NOPE - QK_ROPE
    z = lambda *s: jnp.zeros(s, F32)
    base = 2 * CONV_DIM + Q_LORA + KV_LORA
    w_kr = w_in[:, base:base + QK_ROPE]
    grp = lambda w: jnp.concatenate([z(D, QK_NOPE), w, z(D, pad)], axis=1)
    w_in_p = jnp.concatenate([w_in[:, :base], grp(w_kr), grp(_swap_halves(w_kr))], axis=1)

    wq = w_q_b.reshape(Q_LORA, MLA_HEADS, QK_NOPE + QK_ROPE)
    wq_nope, wq_rope = wq[..., :QK_NOPE], wq[..., QK_NOPE:]
    zq = z(Q_LORA, MLA_HEADS, pad)
    wq_p = jnp.concatenate([wq_nope, wq_rope, zq], axis=-1).reshape(Q_LORA, -1)
    wq_sw = jnp.concatenate([jnp.zeros_like(wq_nope), _swap_halves(wq_rope), zq], axis=-1).reshape(Q_LORA, -1)
    wq_all = jnp.concatenate([wq_p, wq_sw], axis=1)

    wkv = w_kv_b.reshape(KV_LORA, MLA_HEADS, QK_NOPE + V_HEAD)
    wk_p = jnp.concatenate([wkv[..., :QK_NOPE], z(KV_LORA, MLA_HEADS, MLA_HEAD_PAD - QK_NOPE)],
                           axis=-1).reshape(KV_LORA, -1)
    wv_p = wkv[..., QK_NOPE:].reshape(KV_LORA, -1)

    lane = jnp.arange(MLA_HEAD_PAD)
    in_nope = lane < QK_NOPE
    in_rope = (lane >= QK_NOPE) & (lane < QK_NOPE + QK_ROPE)
    seg = (jnp.where(in_nope[:, None] & in_nope[None, :], 1.0 / QK_NOPE, 0.0)
           + jnp.where(in_rope[:, None] & in_rope[None, :], 1.0 / QK_ROPE, 0.0))
    vec = lambda nope, rope: jnp.concatenate([nope, rope, z(pad)]).reshape(1, MLA_HEAD_PAD)
    pair = lambda v: jnp.concatenate([v, v], axis=1)
    gq = pair(vec(q_nope_norm, q_rope_norm))
    gqs = pair(vec(jnp.zeros_like(q_nope_norm), _swap_halves(q_rope_norm)))
    gk = pair(vec(k_nope_norm, z(QK_ROPE)))
    gkr = vec(z(QK_NOPE), k_rope_norm)
    gkrs = vec(z(QK_NOPE), _swap_halves(k_rope_norm))
    zs = jnp.zeros_like(seg)
    seg2 = jnp.concatenate([jnp.concatenate([seg, zs], axis=1), jnp.concatenate([zs, seg], axis=1)], axis=0)
    return (w_in_p.astype(BF16), wq_all.astype(BF16), wk_p.astype(BF16), wv_p.astype(BF16),
            seg2.astype(BF16), gq, gqs, gk, gkr, gkrs)


def kernel(x, mem, positions, ffn1_norm, ffn1_w_gate, ffn1_w_up, ffn1_w_down, ffn2_norm, ffn2_w_gate, ffn2_w_up, ffn2_w_down, mix_norm, xattn_norm, mem_norm, xattn_wq, xattn_wk, xattn_wv, xattn_wo, xattn_q_norm, xattn_k_norm, ev_w_in, ev_conv_w, ev_conv_b, ev_conv_ln_g, ev_conv_ln_b, ev_q_a_norm, ev_w_q_b, ev_kv_a_norm, ev_w_kv_b, ev_q_nope_norm, ev_k_nope_norm, ev_q_rope_norm, ev_k_rope_norm, ev_w_out, od_w_in, od_gn_g, od_gn_b, od_w_out):
    B, S, D = x.shape
    depth = ffn1_norm.shape[0]
    tm = min(TOKEN_TILE, S)
    tf = min(FFN_TOKEN_TILE, B * S)
    assert S % tm == 0 and (B * S) % tf == 0 and S % min(tf, S) == 0 and tm % RET_BLOCK == 0 and tm % CONV_HALO == 0
    bf = lambda w: w.astype(BF16)

    h = x.reshape(B * S, D)
    cos_r, sin_r, cm, sm = _rope_tables(positions, tm)
    decay, xi, zeta, gch = _retention_constants()

    for l in range(depth):
        h = _ffn(h, ffn1_norm[l], bf(ffn1_w_gate[l]), bf(ffn1_w_up[l]), bf(ffn1_w_down[l]), tf)
        if l % 2 == 0:
            e = l // 2
            (w_in_p, wq_p, wk_p, wv_p, seg, gq, gqs, gk, gkr, gkrs) = _mla_weights(
                ev_w_in[e], ev_w_q_b[e], ev_w_kv_b[e], ev_q_nope_norm[e], ev_k_nope_norm[e],
                ev_q_rope_norm[e], ev_k_rope_norm[e])
            a, q, k, v = _even_pre(h, mix_norm[l], w_in_p, cm, sm, ev_q_a_norm[e].reshape(1, -1), wq_p,
                                   ev_kv_a_norm[e].reshape(1, -1), wk_p, wv_p, seg, gq, gqs, gk, gkr,
                                   gkrs, tm)
            m = _mla_attention(q, k, v, B, S, tm)
            h = _even_post(h, a, m, ev_conv_w[e], ev_conv_b[e], ev_conv_ln_g[e], ev_conv_ln_b[e],
                           bf(ev_w_out[e][:CONV_DIM]), bf(ev_w_out[e][CONV_DIM:]), tm, S)
        else:
            o = l // 2
            q, k, v, gate = _odd_pre(h, mix_norm[l], bf(od_w_in[o]), cos_r, sin_r, tm)
            h = _retention(h, q, k, v, gate, decay, xi, zeta, gch, od_gn_g[o], od_gn_b[o],
                           bf(od_w_out[o]), B, S, tm)
        kt, vm = _memkv(mem, mem_norm[l], bf(xattn_wk[l]), bf(xattn_wv[l]), xattn_k_norm[l])
        h = _xattn(h, xattn_norm[l], bf(xattn_wq[l]), xattn_q_norm[l], kt, vm, bf(xattn_wo[l]), min(tf, S), S)
        h = _ffn(h, ffn2_norm[l], bf(ffn2_w_gate[l]), bf(ffn2_w_up[l]), bf(ffn2_w_down[l]), tf)
    return h.reshape(B, S, D)
```

```python
import functools
import math

import jax
import jax.numpy as jnp
from jax import lax
from jax.experimental import pallas as pl
from jax.experimental.pallas import tpu as pltpu

F32 = jnp.float32
BF16 = jnp.bfloat16

EPS = 1e-6
ROPE_THETA = 10000.0

D_FF = 2816
CONV_DIM = 512
CONV_WIDTH = 31
MLA_HEADS = 8
Q_LORA = 256
KV_LORA = 128
QK_NOPE = 64
QK_ROPE = 32
V_HEAD = 64
RET_HEADS = 4
RET_QK = 256
RET_V = 512
X_HEADS = 4

V7X_LANES = 128
V7X_MXU_DIM = 256
V7X_VMEM_LIMIT_BYTES = 56 * 1024 * 1024

TOKEN_TILE = 512
FFN_TOKEN_TILE = 1024
RET_BLOCK = 256
CONV_HALO = 32
MLA_HEAD_PAD = 128


def _params(*sem):
    return pltpu.CompilerParams(dimension_semantics=sem, vmem_limit_bytes=V7X_VMEM_LIMIT_BYTES)


def _const_spec(shape):
    nd = len(shape)
    return pl.BlockSpec(shape, lambda *_: (0,) * nd, pipeline_mode=pl.Buffered(1))


def _rms(x, gain):
    return x * lax.rsqrt(jnp.mean(x * x, axis=-1, keepdims=True) + EPS) * gain


def _silu(x):
    return x / (1.0 + jnp.exp(-x))


def _dot(a, b):
    return jnp.dot(a, b, preferred_element_type=F32)


def _dot_nt(a, b):
    return lax.dot_general(a, b, (((1,), (1,)), ((), ())), preferred_element_type=F32)


def _dot_tn(a, b):
    return lax.dot_general(a, b, (((0,), (0,)), ((), ())), preferred_element_type=F32)


def _select_lanes(t, sel):
    hi = t.astype(BF16)
    rest = t - hi.astype(F32)
    mid = rest.astype(BF16)
    lo = (rest - mid.astype(F32)).astype(BF16)
    return _dot(hi, sel) + _dot(mid, sel) + _dot(lo, sel)


def _rope_kernel(pos_ref, inv_ref, selc_ref, sels_ref, one_ref, cos_ref, sin_ref, cm_ref, sm_ref):
    ang = pos_ref[...].astype(F32) * inv_ref[...]
    cos = jnp.cos(ang)
    sin = jnp.sin(ang)
    cos_ref[...] = cos
    sin_ref[...] = sin
    cm_ref[...] = _select_lanes(cos, selc_ref[...]) + one_ref[...]
    sm_ref[...] = _select_lanes(sin, sels_ref[...])


def _rope_tables(positions, tm):
    T = positions.size
    inv = (ROPE_THETA ** (-jnp.arange(0, RET_QK, 2, dtype=F32) / RET_QK)).reshape(1, RET_QK // 2)
    assert RET_QK % QK_ROPE == 0 and RET_QK // 2 == V7X_LANES
    half = QK_ROPE // 2
    src = jnp.arange(half) * (RET_QK // QK_ROPE)
    lane = jnp.arange(V7X_LANES)
    hit = lambda dst0: (lane[:, None] == src[None, :]).astype(F32) @ \
        (jnp.arange(half)[:, None] + dst0 == lane[None, :]).astype(F32)
    first, second = hit(QK_NOPE), hit(QK_NOPE + half)
    sel_cos = (first + second).astype(BF16)
    sel_sin = (second - first).astype(BF16)
    one = ((lane < QK_NOPE) | (lane >= QK_NOPE + QK_ROPE)).astype(F32).reshape(1, V7X_LANES)
    row = pl.BlockSpec((tm, V7X_LANES), lambda i: (i, 0))
    mat = _const_spec((V7X_LANES, V7X_LANES))
    out = jax.ShapeDtypeStruct((T, V7X_LANES), F32)
    return pl.pallas_call(
        _rope_kernel,
        grid=(T // tm,),
        in_specs=[pl.BlockSpec((tm, 1), lambda i: (i, 0)), _const_spec((1, V7X_LANES)), mat, mat,
                  _const_spec((1, V7X_LANES))],
        out_specs=[row, row, row, row],
        out_shape=[out, out, out, out],
        compiler_params=_params("parallel"),
        name="rope_tables",
    )(positions.reshape(T, 1), inv, sel_cos, sel_sin, one)


def _ff_chunks(d_ff, width):
    edges = list(range(0, d_ff, width)) + [d_ff]
    return [(a, b) for a, b in zip(edges[:-1], edges[1:])]


def _ffn_kernel(x_ref, g_ref, wg_ref, wu_ref, wd_ref, o_ref, h_ref):
    x = x_ref[...]
    xn = _rms(x, g_ref[...]).astype(BF16)
    for a, b in _ff_chunks(h_ref.shape[1], 2 * V7X_MXU_DIM):
        gate = _dot(xn, wg_ref[:, a:b])
        up = _dot(xn, wu_ref[:, a:b])
        h_ref[:, a:b] = (_silu(gate) * up).astype(BF16)
    o_ref[...] = x + 0.5 * _dot(h_ref[...], wd_ref[...])


def _ffn(x, gain, w_gate, w_up, w_down, tm):
    T, D = x.shape
    d_ff = w_gate.shape[1]
    row = pl.BlockSpec((tm, D), lambda i: (i, 0))
    return pl.pallas_call(
        _ffn_kernel,
        grid=(T // tm,),
        in_specs=[row, _const_spec((1, D)), _const_spec((D, d_ff)), _const_spec((D, d_ff)),
                  _const_spec((d_ff, D))],
        out_specs=row,
        out_shape=jax.ShapeDtypeStruct((T, D), F32),
        scratch_shapes=[pltpu.VMEM((tm, d_ff), BF16)],
        compiler_params=_params("parallel"),
        name="ffn",
    )(x, gain.reshape(1, D), w_gate, w_up, w_down)


def _memkv_kernel(mem_ref, g_ref, wk_ref, wv_ref, kn_ref, kt_ref, v_ref):
    mn = _rms(mem_ref[...], g_ref[...]).astype(BF16)
    k = _dot(mn, wk_ref[...])
    hd = kn_ref.shape[1]
    for h in range(X_HEADS):
        kh = _rms(k[:, h * hd:(h + 1) * hd], kn_ref[...])
        kt_ref[0, h * hd:(h + 1) * hd, :] = kh.T.astype(BF16)
    v_ref[0] = _dot(mn, wv_ref[...]).astype(BF16)


def _memkv(mem, gain, wk, wv, k_norm):
    B, M, D = mem.shape
    hd = D // X_HEADS
    return pl.pallas_call(
        _memkv_kernel,
        grid=(B,),
        in_specs=[pl.BlockSpec((M, D), lambda b: (b, 0)), _const_spec((1, D)),
                  _const_spec((D, D)), _const_spec((D, D)), _const_spec((1, hd))],
        out_specs=[pl.BlockSpec((1, D, M), lambda b: (b, 0, 0)),
                   pl.BlockSpec((1, M, D), lambda b: (b, 0, 0))],
        out_shape=[jax.ShapeDtypeStruct((B, D, M), BF16), jax.ShapeDtypeStruct((B, M, D), BF16)],
        compiler_params=_params("parallel"),
        name="xattn_memkv",
    )(mem.reshape(B * M, D), gain.reshape(1, D), wk, wv, k_norm.reshape(1, hd))


def _xattn_kernel(x_ref, g_ref, wq_ref, qn_ref, kt_ref, v_ref, wo_ref, o_ref, att_ref):
    x = x_ref[...]
    xn = _rms(x, g_ref[...]).astype(BF16)
    q = _dot(xn, wq_ref[...])
    hd = qn_ref.shape[1]
    scale = hd ** -0.5
    for h in range(X_HEADS):
        sl = slice(h * hd, (h + 1) * hd)
        qh = (_rms(q[:, sl], qn_ref[...]) * scale).astype(BF16)
        s = _dot(qh, kt_ref[0, sl, :])
        p = jnp.exp(s - jnp.max(s, axis=-1, keepdims=True))
        l = jnp.sum(p, axis=-1, keepdims=True)
        att_ref[:, sl] = (_dot(p.astype(BF16), v_ref[0, :, sl]) / l).astype(BF16)
    o_ref[...] = x + _dot(att_ref[...], wo_ref[...])


def _xattn(x, gain, wq, q_norm, kt, v, wo, tm, seq):
    T, D = x.shape
    M = v.shape[1]
    hd = D // X_HEADS
    per_seq = seq // tm
    row = pl.BlockSpec((tm, D), lambda i: (i, 0))
    return pl.pallas_call(
        _xattn_kernel,
        grid=(T // tm,),
        in_specs=[row, _const_spec((1, D)), _const_spec((D, D)), _const_spec((1, hd)),
                  pl.BlockSpec((1, D, M), lambda i: (i // per_seq, 0, 0)),
                  pl.BlockSpec((1, M, D), lambda i: (i // per_seq, 0, 0)),
                  _const_spec((D, D))],
        out_specs=row,
        out_shape=jax.ShapeDtypeStruct((T, D), F32),
        scratch_shapes=[pltpu.VMEM((tm, D), BF16)],
        compiler_params=_params("parallel"),
        name="xattn",
    )(x, gain.reshape(1, D), wq, q_norm.reshape(1, hd), kt, v, wo)


def _even_pre_kernel(x_ref, g_ref, win_ref, cm_ref, sm_ref, qa_ref, wq_ref, kva_ref, wk_ref, wv_ref,
                     seg_ref, gq_ref, gqs_ref, gk_ref, gkr_ref, gkrs_ref,
                     a_ref, qt_ref, k_ref, vt_ref):
    xn = _rms(x_ref[...], g_ref[...]).astype(BF16)
    c = CONV_DIM
    glu = _dot(xn, win_ref[:, 0:2 * c])
    a_ref[...] = glu[:, :c] / (1.0 + jnp.exp(-glu[:, c:]))
    hp = MLA_HEAD_PAD
    z = _dot(xn, win_ref[:, 2 * c:])
    zq = z[:, :Q_LORA]
    zkv = z[:, Q_LORA:Q_LORA + KV_LORA]
    zkr = z[:, Q_LORA + KV_LORA:Q_LORA + KV_LORA + hp]
    zkr_sw = z[:, Q_LORA + KV_LORA + hp:]
    cm = cm_ref[...]
    sm = sm_ref[...]

    rs_kr = lax.rsqrt(jnp.sum(zkr * zkr, axis=-1, keepdims=True) * (1.0 / QK_ROPE) + EPS)
    k_rope = (zkr * rs_kr * gkr_ref[...]) * cm + (zkr_sw * rs_kr * gkrs_ref[...]) * sm

    zq_n = _rms(zq, qa_ref[...]).astype(BF16)
    zkv_n = _rms(zkv, kva_ref[...]).astype(BF16)
    vt_ref[0] = _dot(zkv_n, wv_ref[...]).T.astype(BF16)
    pair = lambda t: jnp.concatenate([t, t], axis=1)
    cm2, sm2, k_rope2 = pair(cm), pair(sm), pair(k_rope)
    pw = 2 * hp
    nq = MLA_HEADS * hp
    seg = seg_ref[...]
    for h in range(MLA_HEADS // 2):
        sl = slice(h * pw, (h + 1) * pw)
        qh = _dot(zq_n, wq_ref[:, sl])
        qh_sw = _dot(zq_n, wq_ref[:, nq + h * pw:nq + (h + 1) * pw])
        rs = lax.rsqrt(_dot((qh * qh).astype(BF16), seg) + EPS)
        qr = (qh * rs * gq_ref[...]) * cm2 + (qh_sw * rs * gqs_ref[...]) * sm2
        qt_ref[0, sl, :] = (qr * MLA_Q_SCALE).T.astype(BF16)
        kh = _dot(zkv_n, wk_ref[:, sl])
        rk = lax.rsqrt(_dot((kh * kh).astype(BF16), seg) + EPS)
        k_ref[:, sl] = (kh * rk * gk_ref[...] + k_rope2).astype(BF16)


def _even_pre(x, gain, w_in_p, cm, sm, qa, wq_p, kva, wk_p, wv_p, seg, gq, gqs, gk, gkr, gkrs, tm):
    T, D = x.shape
    row = lambda w: pl.BlockSpec((tm, w), lambda i: (i, 0))
    hq, hv = MLA_HEADS * MLA_HEAD_PAD, MLA_HEADS * V_HEAD
    consts = [gain.reshape(1, D), w_in_p]
    tables = [cm, sm]
    rest = [qa, wq_p, kva, wk_p, wv_p, seg, gq, gqs, gk, gkr, gkrs]
    return pl.pallas_call(
        _even_pre_kernel,
        grid=(T // tm,),
        in_specs=[row(D)] + [_const_spec(a.shape) for a in consts] + [row(V7X_LANES)] * 2
                 + [_const_spec(a.shape) for a in rest],
        out_specs=[row(CONV_DIM), pl.BlockSpec((1, hq, tm), lambda i: (i, 0, 0)), row(hq),
                   pl.BlockSpec((1, hv, tm), lambda i: (i, 0, 0))],
        out_shape=[jax.ShapeDtypeStruct((T, CONV_DIM), F32), jax.ShapeDtypeStruct((T // tm, hq, tm), BF16),
                   jax.ShapeDtypeStruct((T, hq), BF16), jax.ShapeDtypeStruct((T // tm, hv, tm), BF16)],
        compiler_params=_params("parallel"),
        name="even_pre",
    )(x, *consts, *tables, *rest)


MLA_HEADS_PER_STEP = 8
MLA_Q_SCALE = (QK_NOPE + QK_ROPE) ** -0.5 * math.log2(math.e)


MLA_SUM_ROWS = 16


def _mla_kernel(qt_ref, k_ref, vt_ref, o_ref, m_ref, acc_ref, s_ref):
    qi = pl.program_id(2)
    tq = qt_ref.shape[2]
    hp = MLA_HEAD_PAD
    ones = jnp.ones((MLA_SUM_ROWS, tq), BF16)

    m_ref[...] = jnp.full(m_ref.shape, -jnp.inf, F32)
    acc_ref[...] = jnp.zeros(acc_ref.shape, F32)

    def block(j, masked):
        rows = pl.ds(pl.multiple_of(j * tq, tq), tq)
        for h in range(MLA_HEADS_PER_STEP):
            s_ref[h] = _dot(k_ref[rows, h * hp:(h + 1) * hp], qt_ref[0, h * hp:(h + 1) * hp, :])
        for h in range(MLA_HEADS_PER_STEP):
            s = s_ref[h]
            if masked:
                key = lax.broadcasted_iota(jnp.int32, s.shape, 0)
                qry = lax.broadcasted_iota(jnp.int32, s.shape, 1)
                s = jnp.where(key <= qry, s, -jnp.inf)
            m_old = m_ref[h]
            m_new = jnp.maximum(m_old, jnp.max(s, axis=0, keepdims=True))
            p = jnp.exp2(s - m_new).astype(BF16)
            v1 = jnp.concatenate([vt_ref[j, h * V_HEAD:(h + 1) * V_HEAD, :], ones], axis=0)
            acc_ref[h] = jnp.exp2(m_old - m_new) * acc_ref[h] + _dot(v1, p)
            m_ref[h] = m_new

    def body(j, carry):
        block(j, False)
        return carry

    lax.fori_loop(0, qi, body, 0)
    block(qi, True)
    out = [acc_ref[h, 0:V_HEAD, :] / acc_ref[h, V_HEAD:V_HEAD + 1, :] for h in range(MLA_HEADS_PER_STEP)]
    o_ref[...] = jnp.concatenate(out, axis=0).T.astype(BF16)


def _mla_attention(qt, k, vt, batch, seq, tq):
    T = k.shape[0]
    hs = MLA_HEADS_PER_STEP
    nq = seq // tq
    qw, vw = hs * MLA_HEAD_PAD, hs * V_HEAD
    return pl.pallas_call(
        _mla_kernel,
        grid=(batch, MLA_HEADS // hs, nq),
        in_specs=[pl.BlockSpec((1, qw, tq), lambda b, h, i: (b * nq + i, h, 0)),
                  pl.BlockSpec((seq, qw), lambda b, h, i: (b, h)),
                  pl.BlockSpec((nq, vw, tq), lambda b, h, i: (b, h, 0))],
        out_specs=pl.BlockSpec((tq, vw), lambda b, h, i: (b * nq + i, h)),
        out_shape=jax.ShapeDtypeStruct((T, MLA_HEADS * V_HEAD), BF16),
        scratch_shapes=[pltpu.VMEM((hs, 1, tq), F32), pltpu.VMEM((hs, V_HEAD + MLA_SUM_ROWS, tq), F32),
                        pltpu.VMEM((hs, tq, tq), F32)],
        compiler_params=_params("parallel", "parallel", "arbitrary"),
        name="mla_attention",
    )(qt, k, vt)


CONV_ROWS = 128
V7X_SUBLANES = 8


def _causal_conv(win_ref, cw_ref, ts):
    off = CONV_HALO - (CONV_WIDTH - 1)
    sub = V7X_SUBLANES
    lane_blocks = []
    for lb in range(win_ref.shape[1] // V7X_LANES):
        lanes = slice(lb * V7X_LANES, (lb + 1) * V7X_LANES)
        row_blocks = []
        for base in range(0, ts, CONV_ROWS):
            y = None
            for r in range(sub):
                n = CONV_ROWS + (sub if r else 0)
                z = None
                for q in range(-(-(off + CONV_WIDTH) // sub)):
                    j = sub * q + r - off
                    if 0 <= j < CONV_WIDTH:
                        lo = base + sub * q
                        term = cw_ref[j:j + 1, lanes] * win_ref[lo:lo + n, lanes]
                        z = term if z is None else z + term
                z = z[r:r + CONV_ROWS]
                y = z if y is None else y + z
            row_blocks.append(y)
        lane_blocks.append(jnp.concatenate(row_blocks, axis=0))
    return jnp.concatenate(lane_blocks, axis=1)


def _even_post_kernel(x_ref, halo_ref, a_ref, m_ref, cw_ref, cb_ref, lg_ref, lb_ref, wa_ref, wm_ref,
                      o_ref, win_ref, *, per_seq):
    ts = a_ref.shape[0]
    first = (pl.program_id(0) % per_seq) == 0
    win_ref[0:CONV_HALO, :] = jnp.where(first, 0.0, halo_ref[...])
    win_ref[CONV_HALO:, :] = a_ref[...]
    y = _causal_conv(win_ref, cw_ref, ts) + cb_ref[...]
    mu = jnp.mean(y, axis=-1, keepdims=True)
    yc = y - mu
    var = jnp.mean(yc * yc, axis=-1, keepdims=True)
    act = _silu(yc * lax.rsqrt(var + EPS) * lg_ref[...] + lb_ref[...]).astype(BF16)
    o_ref[...] = x_ref[...] + _dot(act, wa_ref[...]) + _dot(m_ref[...], wm_ref[...])


def _even_post(x, a, m, conv_w, conv_b, ln_g, ln_b, w_out_a, w_out_m, ts, seq):
    T, D = x.shape
    C = a.shape[1]
    per_seq = seq // ts
    halo_blocks = ts // CONV_HALO
    row = lambda w: pl.BlockSpec((ts, w), lambda i: (i, 0))
    return pl.pallas_call(
        functools.partial(_even_post_kernel, per_seq=per_seq),
        grid=(T // ts,),
        in_specs=[row(D),
                  pl.BlockSpec((CONV_HALO, C), lambda i: (jnp.maximum(i * halo_blocks - 1, 0), 0)),
                  row(C), row(C), _const_spec((CONV_WIDTH, C)), _const_spec((1, C)),
                  _const_spec((1, C)), _const_spec((1, C)), _const_spec((C, D)), _const_spec((C, D))],
        out_specs=row(D),
        out_shape=jax.ShapeDtypeStruct((T, D), F32),
        scratch_shapes=[pltpu.VMEM((CONV_HALO + ts, C), F32)],
        compiler_params=_params("parallel"),
        name="even_post",
    )(x, a, a, m, conv_w, conv_b.reshape(1, C), ln_g.reshape(1, C), ln_b.reshape(1, C),
      w_out_a, w_out_m)


def _odd_pre_kernel(x_ref, g_ref, w_ref, cos_ref, sin_ref, q_ref, kt_ref, v_ref, gate_ref):
    xn = _rms(x_ref[...], g_ref[...]).astype(BF16)
    cos = cos_ref[...]
    sin = sin_ref[...]
    half = RET_QK // 2
    nqk = RET_HEADS * RET_QK
    k_scale = RET_QK ** -0.5
    for h in range(RET_HEADS):
        lo = h * RET_QK
        z = _dot(xn, w_ref[:, lo:lo + RET_QK])
        x1, x2 = z[:, :half], z[:, half:]
        q_ref[:, lo:lo + half] = (x1 * cos - x2 * sin).astype(BF16)
        q_ref[:, lo + half:lo + RET_QK] = (x1 * sin + x2 * cos).astype(BF16)
        z = _dot(xn, w_ref[:, nqk + lo:nqk + lo + RET_QK])
        x1, x2 = z[:, :half], z[:, half:]
        k = jnp.concatenate([x1 * cos - x2 * sin, x1 * sin + x2 * cos], axis=1) * k_scale
        kt_ref[0, lo:lo + RET_QK, :] = k.T.astype(BF16)
    nv = RET_HEADS * RET_V
    for h in range(RET_HEADS):
        sl = slice(h * RET_V, (h + 1) * RET_V)
        v_ref[:, sl] = _dot(xn, w_ref[:, 2 * nqk + h * RET_V:2 * nqk + (h + 1) * RET_V]).astype(BF16)
        gate_ref[:, sl] = _dot(xn, w_ref[:, 2 * nqk + nv + h * RET_V:2 * nqk + nv + (h + 1) * RET_V])


def _odd_pre(x, gain, w_in, cos_r, sin_r, tm):
    T, D = x.shape
    nqk, nv = RET_HEADS * RET_QK, RET_HEADS * RET_V
    row = lambda w: pl.BlockSpec((tm, w), lambda i: (i, 0))
    return pl.pallas_call(
        _odd_pre_kernel,
        grid=(T // tm,),
        in_specs=[row(D), _const_spec((1, D)), _const_spec(w_in.shape), row(V7X_LANES), row(V7X_LANES)],
        out_specs=[row(nqk), pl.BlockSpec((1, nqk, tm), lambda i: (i, 0, 0)), row(nv), row(nv)],
        out_shape=[jax.ShapeDtypeStruct((T, nqk), BF16), jax.ShapeDtypeStruct((T // tm, nqk, tm), BF16),
                   jax.ShapeDtypeStruct((T, nv), BF16), jax.ShapeDtypeStruct((T, nv), F32)],
        compiler_params=_params("parallel"),
        name="odd_pre",
    )(x, gain.reshape(1, D), w_in, cos_r, sin_r)


def _retention_kernel(x_ref, q_ref, kt_ref, v_ref, gate_ref, decay_ref, xi_ref, zeta_ref, gch_ref,
                      gng_ref, gnb_ref, wo_ref, o_ref, state_ref, y_ref):
    @pl.when(pl.program_id(1) == 0)
    def _():
        state_ref[...] = jnp.zeros(state_ref.shape, F32)

    ts = x_ref.shape[0]
    C = RET_BLOCK
    for c in range(ts // C):
        rows = slice(c * C, (c + 1) * C)
        for h in range(RET_HEADS):
            qk = slice(h * RET_QK, (h + 1) * RET_QK)
            vv = slice(h * RET_V, (h + 1) * RET_V)
            qc = q_ref[rows, qk]
            kt = kt_ref[0, qk, rows]
            vc = v_ref[rows, vv]
            state = state_ref[h]
            scores = (_dot(qc, kt) * decay_ref[h]).astype(BF16)
            out = _dot(scores, vc) + _dot(qc, state.astype(BF16)) * xi_ref[h]
            kz = (kt.astype(F32) * zeta_ref[h]).astype(BF16)
            state_ref[h] = state * gch_ref[h] + _dot(kz, vc)
            mu = jnp.mean(out, axis=-1, keepdims=True)
            oc = out - mu
            var = jnp.mean(oc * oc, axis=-1, keepdims=True)
            gn = oc * lax.rsqrt(var + EPS) * gng_ref[:, vv] + gnb_ref[:, vv]
            y_ref[rows, vv] = (_silu(gate_ref[rows, vv]) * gn).astype(BF16)
    o_ref[...] = x_ref[...] + _dot(y_ref[...], wo_ref[...])


def _retention(x, q, k, v, gate, decay, xi, zeta, gch, gn_g, gn_b, w_out, batch, seq, ts):
    T, D = x.shape
    nqk, nv = RET_HEADS * RET_QK, RET_HEADS * RET_V
    per_seq = seq // ts
    row = lambda w: pl.BlockSpec((ts, w), lambda b, i: (b * per_seq + i, 0))
    return pl.pallas_call(
        _retention_kernel,
        grid=(batch, per_seq),
        in_specs=[row(D), row(nqk), pl.BlockSpec((1, nqk, ts), lambda b, i: (b * per_seq + i, 0, 0)),
                  row(nv), row(nv), _const_spec(decay.shape), _const_spec(xi.shape), _const_spec(zeta.shape),
                  _const_spec(gch.shape), _const_spec((1, nv)), _const_spec((1, nv)),
                  _const_spec((nv, D))],
        out_specs=row(D),
        out_shape=jax.ShapeDtypeStruct((T, D), F32),
        scratch_shapes=[pltpu.VMEM((RET_HEADS, RET_QK, RET_V), F32), pltpu.VMEM((ts, nv), BF16)],
        compiler_params=_params("parallel", "arbitrary"),
        name="retention",
    )(x, q, k, v, gate, decay, xi, zeta, gch, gn_g.reshape(1, nv), gn_b.reshape(1, nv), w_out)


def _retention_constants():
    H, C = RET_HEADS, RET_BLOCK
    log_g = jnp.log1p(-jnp.exp2(-5.0 - jnp.arange(H, dtype=F32)))
    idx = jnp.arange(C, dtype=F32)
    diff = idx[:, None] - idx[None, :]
    decay = jnp.where(diff >= 0, jnp.exp(log_g[:, None, None] * jnp.maximum(diff, 0.0)), 0.0)
    xi = jnp.exp(log_g[:, None] * (idx + 1.0))[:, :, None]
    zeta = jnp.exp(log_g[:, None] * (C - 1.0 - idx))[:, None, :]
    gch = jnp.exp(log_g * C)[:, None, None]
    return decay, xi, zeta, gch


def _swap_halves(w):
    half = w.shape[-1] // 2
    return jnp.concatenate([w[..., half:], w[..., :half]], axis=-1)


def _mla_weights(w_in, w_q_b, w_kv_b, q_nope_norm, k_nope_norm, q_rope_norm, k_rope_norm):
    D = w_in.shape[0]
    pad = MLA_HEAD_PAD - QK_NOPE - QK_ROPE
    z = lambda *s: jnp.zeros(s, F32)
    base = 2 * CONV_DIM + Q_LORA + KV_LORA
    w_kr = w_in[:, base:base + QK_ROPE]
    grp = lambda w: jnp.concatenate([z(D, QK_NOPE), w, z(D, pad)], axis=1)
    w_in_p = jnp.concatenate([w_in[:, :base], grp(w_kr), grp(_swap_halves(w_kr))], axis=1)

    wq = w_q_b.reshape(Q_LORA, MLA_HEADS, QK_NOPE + QK_ROPE)
    wq_nope, wq_rope = wq[..., :QK_NOPE], wq[..., QK_NOPE:]
    zq = z(Q_LORA, MLA_HEADS, pad)
    wq_p = jnp.concatenate([wq_nope, wq_rope, zq], axis=-1).reshape(Q_LORA, -1)
    wq_sw = jnp.concatenate([jnp.zeros_like(wq_nope), _swap_halves(wq_rope), zq], axis=-1).reshape(Q_LORA, -1)
    wq_all = jnp.concatenate([wq_p, wq_sw], axis=1)

    wkv = w_kv_b.reshape(KV_LORA, MLA_HEADS, QK_NOPE + V_HEAD)
    wk_p = jnp.concatenate([wkv[..., :QK_NOPE], z(KV_LORA, MLA_HEADS, MLA_HEAD_PAD - QK_NOPE)],
                           axis=-1).reshape(KV_LORA, -1)
    wv_p = wkv[..., QK_NOPE:].reshape(KV_LORA, -1)

    lane = jnp.arange(MLA_HEAD_PAD)
    in_nope = lane < QK_NOPE
    in_rope = (lane >= QK_NOPE) & (lane < QK_NOPE + QK_ROPE)
    seg = (jnp.where(in_nope[:, None] & in_nope[None, :], 1.0 / QK_NOPE, 0.0)
           + jnp.where(in_rope[:, None] & in_rope[None, :], 1.0 / QK_ROPE, 0.0))
    vec = lambda nope, rope: jnp.concatenate([nope, rope, z(pad)]).reshape(1, MLA_HEAD_PAD)
    pair = lambda v: jnp.concatenate([v, v], axis=1)
    gq = pair(vec(q_nope_norm, q_rope_norm))
    gqs = pair(vec(jnp.zeros_like(q_nope_norm), _swap_halves(q_rope_norm)))
    gk = pair(vec(k_nope_norm, z(QK_ROPE)))
    gkr = vec(z(QK_NOPE), k_rope_norm)
    gkrs = vec(z(QK_NOPE), _swap_halves(k_rope_norm))
    zs = jnp.zeros_like(seg)
    seg2 = jnp.concatenate([jnp.concatenate([seg, zs], axis=1), jnp.concatenate([zs, seg], axis=1)], axis=0)
    return (w_in_p.astype(BF16), wq_all.astype(BF16), wk_p.astype(BF16), wv_p.astype(BF16),
            seg2.astype(BF16), gq, gqs, gk, gkr, gkrs)


def kernel(x, mem, positions, ffn1_norm, ffn1_w_gate, ffn1_w_up, ffn1_w_down, ffn2_norm, ffn2_w_gate, ffn2_w_up, ffn2_w_down, mix_norm, xattn_norm, mem_norm, xattn_wq, xattn_wk, xattn_wv, xattn_wo, xattn_q_norm, xattn_k_norm, ev_w_in, ev_conv_w, ev_conv_b, ev_conv_ln_g, ev_conv_ln_b, ev_q_a_norm, ev_w_q_b, ev_kv_a_norm, ev_w_kv_b, ev_q_nope_norm, ev_k_nope_norm, ev_q_rope_norm, ev_k_rope_norm, ev_w_out, od_w_in, od_gn_g, od_gn_b, od_w_out):
    B, S, D = x.shape
    depth = ffn1_norm.shape[0]
    tm = min(TOKEN_TILE, S)
    tf = min(FFN_TOKEN_TILE, B * S)
    assert S % tm == 0 and (B * S) % tf == 0 and S % min(tf, S) == 0 and tm % RET_BLOCK == 0 and tm % CONV_HALO == 0
    bf = lambda w: w.astype(BF16)

    h = x.reshape(B * S, D)
    cos_r, sin_r, cm, sm = _rope_tables(positions, tm)
    decay, xi, zeta, gch = _retention_constants()

    for l in range(depth):
        h = _ffn(h, ffn1_norm[l], bf(ffn1_w_gate[l]), bf(ffn1_w_up[l]), bf(ffn1_w_down[l]), tf)
        if l % 2 == 0:
            e = l // 2
            (w_in_p, wq_p, wk_p, wv_p, seg, gq, gqs, gk, gkr, gkrs) = _mla_weights(
                ev_w_in[e], ev_w_q_b[e], ev_w_kv_b[e], ev_q_nope_norm[e], ev_k_nope_norm[e],
                ev_q_rope_norm[e], ev_k_rope_norm[e])
            a, q, k, v = _even_pre(h, mix_norm[l], w_in_p, cm, sm, ev_q_a_norm[e].reshape(1, -1), wq_p,
                                   ev_kv_a_norm[e].reshape(1, -1), wk_p, wv_p, seg, gq, gqs, gk, gkr,
                                   gkrs, tm)
            m = _mla_attention(q, k, v, B, S, tm)
            h = _even_post(h, a, m, ev_conv_w[e], ev_conv_b[e], ev_conv_ln_g[e], ev_conv_ln_b[e],
                           bf(ev_w_out[e][:CONV_DIM]), bf(ev_w_out[e][CONV_DIM:]), tm, S)
        else:
            o = l // 2
            q, k, v, gate = _odd_pre(h, mix_norm[l], bf(od_w_in[o]), cos_r, sin_r, tm)
            h = _retention(h, q, k, v, gate, decay, xi, zeta, gch, od_gn_g[o], od_gn_b[o],
                           bf(od_w_out[o]), B, S, tm)
        kt, vm = _memkv(mem, mem_norm[l], bf(xattn_wk[l]), bf(xattn_wv[l]), xattn_k_norm[l])
        h = _xattn(h, xattn_norm[l], bf(xattn_wq[l]), xattn_q_norm[l], kt, vm, bf(xattn_wo[l]), min(tf, S), S)
        h = _ffn(h, ffn2_norm[l], bf(ffn2_w_gate[l]), bf(ffn2_w_up[l]), bf(ffn2_w_down[l]), tf)
    return h.reshape(B, S, D)
```

```python
import functools
import math

import jax
import jax.numpy as jnp
from jax import lax
from jax.experimental import pallas as pl
from jax.experimental.pallas import tpu as pltpu

F32 = jnp.float32
BF16 = jnp.bfloat16

EPS = 1e-6
ROPE_THETA = 10000.0

D_FF = 2816
CONV_DIM = 512
CONV_WIDTH = 31
MLA_HEADS = 8
Q_LORA = 256
KV_LORA = 128
QK_NOPE = 64
QK_ROPE = 32
V_HEAD = 64
RET_HEADS = 4
RET_QK = 256
RET_V = 512
X_HEADS = 4

V7X_LANES = 128
V7X_MXU_DIM = 256
V7X_VMEM_LIMIT_BYTES = 56 * 1024 * 1024

TOKEN_TILE = 512
FFN_TOKEN_TILE = 1024
RET_BLOCK = 256
CONV_HALO = 32
MLA_HEAD_PAD = 128


def _params(*sem):
    return pltpu.CompilerParams(dimension_semantics=sem, vmem_limit_bytes=V7X_VMEM_LIMIT_BYTES)


def _const_spec(shape):
    nd = len(shape)
    return pl.BlockSpec(shape, lambda *_: (0,) * nd, pipeline_mode=pl.Buffered(1))


def _rms(x, gain):
    return x * lax.rsqrt(jnp.mean(x * x, axis=-1, keepdims=True) + EPS) * gain


def _silu(x):
    return x / (1.0 + jnp.exp(-x))


def _dot(a, b):
    return jnp.dot(a, b, preferred_element_type=F32)


def _dot_nt(a, b):
    return lax.dot_general(a, b, (((1,), (1,)), ((), ())), preferred_element_type=F32)


def _dot_tn(a, b):
    return lax.dot_general(a, b, (((0,), (0,)), ((), ())), preferred_element_type=F32)


def _select_lanes(t, sel):
    hi = t.astype(BF16)
    rest = t - hi.astype(F32)
    mid = rest.astype(BF16)
    lo = (rest - mid.astype(F32)).astype(BF16)
    return _dot(hi, sel) + _dot(mid, sel) + _dot(lo, sel)


def _rope_kernel(pos_ref, inv_ref, selc_ref, sels_ref, one_ref, cos_ref, sin_ref, cm_ref, sm_ref):
    ang = pos_ref[...].astype(F32) * inv_ref[...]
    cos = jnp.cos(ang)
    sin = jnp.sin(ang)
    cos_ref[...] = cos
    sin_ref[...] = sin
    cm_ref[...] = _select_lanes(cos, selc_ref[...]) + one_ref[...]
    sm_ref[...] = _select_lanes(sin, sels_ref[...])


def _rope_tables(positions, tm):
    T = positions.size
    inv = (ROPE_THETA ** (-jnp.arange(0, RET_QK, 2, dtype=F32) / RET_QK)).reshape(1, RET_QK // 2)
    assert RET_QK % QK_ROPE == 0 and RET_QK // 2 == V7X_LANES
    half = QK_ROPE // 2
    src = jnp.arange(half) * (RET_QK // QK_ROPE)
    lane = jnp.arange(V7X_LANES)
    hit = lambda dst0: (lane[:, None] == src[None, :]).astype(F32) @ \
        (jnp.arange(half)[:, None] + dst0 == lane[None, :]).astype(F32)
    first, second = hit(QK_NOPE), hit(QK_NOPE + half)
    sel_cos = (first + second).astype(BF16)
    sel_sin = (second - first).astype(BF16)
    one = ((lane < QK_NOPE) | (lane >= QK_NOPE + QK_ROPE)).astype(F32).reshape(1, V7X_LANES)
    row = pl.BlockSpec((tm, V7X_LANES), lambda i: (i, 0))
    mat = _const_spec((V7X_LANES, V7X_LANES))
    out = jax.ShapeDtypeStruct((T, V7X_LANES), F32)
    return pl.pallas_call(
        _rope_kernel,
        grid=(T // tm,),
        in_specs=[pl.BlockSpec((tm, 1), lambda i: (i, 0)), _const_spec((1, V7X_LANES)), mat, mat,
                  _const_spec((1, V7X_LANES))],
        out_specs=[row, row, row, row],
        out_shape=[out, out, out, out],
        compiler_params=_params("parallel"),
        name="rope_tables",
    )(positions.reshape(T, 1), inv, sel_cos, sel_sin, one)


def _ff_chunks(d_ff, width):
    edges = list(range(0, d_ff, width)) + [d_ff]
    return [(a, b) for a, b in zip(edges[:-1], edges[1:])]


def _ffn_kernel(x_ref, g_ref, wg_ref, wu_ref, wd_ref, o_ref, h_ref):
    x = x_ref[...]
    xn = _rms(x, g_ref[...]).astype(BF16)
    for a, b in _ff_chunks(h_ref.shape[1], 2 * V7X_MXU_DIM):
        gate = _dot(xn, wg_ref[:, a:b])
        up = _dot(xn, wu_ref[:, a:b])
        h_ref[:, a:b] = (_silu(gate) * up).astype(BF16)
    o_ref[...] = x + 0.5 * _dot(h_ref[...], wd_ref[...])


def _ffn(x, gain, w_gate, w_up, w_down, tm):
    T, D = x.shape
    d_ff = w_gate.shape[1]
    row = pl.BlockSpec((tm, D), lambda i: (i, 0))
    return pl.pallas_call(
        _ffn_kernel,
        grid=(T // tm,),
        in_specs=[row, _const_spec((1, D)), _const_spec((D, d_ff)), _const_spec((D, d_ff)),
                  _const_spec((d_ff, D))],
        out_specs=row,
        out_shape=jax.ShapeDtypeStruct((T, D), F32),
        scratch_shapes=[pltpu.VMEM((tm, d_ff), BF16)],
        compiler_params=_params("parallel"),
        name="ffn",
    )(x, gain.reshape(1, D), w_gate, w_up, w_down)


def _memkv_kernel(mem_ref, g_ref, wk_ref, wv_ref, wo_ref, kn_ref, kt_ref, vw_ref):
    M = mem_ref.shape[0]
    mn = _rms(mem_ref[...], g_ref[...]).astype(BF16)
    k = _dot(mn, wk_ref[...])
    v = _dot(mn, wv_ref[...]).astype(BF16)
    hd = kn_ref.shape[1]
    for h in range(X_HEADS):
        sl = slice(h * hd, (h + 1) * hd)
        kh = _rms(k[:, sl], kn_ref[...])
        kt_ref[0, sl, :] = kh.T.astype(BF16)
        vw_ref[0, h * M:(h + 1) * M, :] = _dot(v[:, sl], wo_ref[sl, :]).astype(BF16)


def _memkv(mem, gain, wk, wv, wo, k_norm):
    B, M, D = mem.shape
    hd = D // X_HEADS
    return pl.pallas_call(
        _memkv_kernel,
        grid=(B,),
        in_specs=[pl.BlockSpec((M, D), lambda b: (b, 0)), _const_spec((1, D)),
                  _const_spec((D, D)), _const_spec((D, D)), _const_spec((D, D)), _const_spec((1, hd))],
        out_specs=[pl.BlockSpec((1, D, M), lambda b: (b, 0, 0)),
                   pl.BlockSpec((1, X_HEADS * M, D), lambda b: (b, 0, 0))],
        out_shape=[jax.ShapeDtypeStruct((B, D, M), BF16), jax.ShapeDtypeStruct((B, X_HEADS * M, D), BF16)],
        compiler_params=_params("parallel"),
        name="xattn_memkv",
    )(mem.reshape(B * M, D), gain.reshape(1, D), wk, wv, wo, k_norm.reshape(1, hd))


def _xattn_kernel(x_ref, g_ref, wq_ref, qn_ref, kt_ref, vw_ref, o_ref, p_ref):
    x = x_ref[...]
    xn = _rms(x, g_ref[...]).astype(BF16)
    q = _dot(xn, wq_ref[...])
    hd = qn_ref.shape[1]
    M = kt_ref.shape[2]
    scale = hd ** -0.5
    for h in range(X_HEADS):
        sl = slice(h * hd, (h + 1) * hd)
        qh = (_rms(q[:, sl], qn_ref[...]) * scale).astype(BF16)
        s = _dot(qh, kt_ref[0, sl, :])
        p = jnp.exp(s - jnp.max(s, axis=-1, keepdims=True))
        p_ref[:, h * M:(h + 1) * M] = (p * (1.0 / jnp.sum(p, axis=-1, keepdims=True))).astype(BF16)
    o_ref[...] = x + _dot(p_ref[...], vw_ref[0])


def _xattn(x, gain, wq, q_norm, kt, vw, tm, seq):
    T, D = x.shape
    M = kt.shape[2]
    hd = D // X_HEADS
    per_seq = seq // tm
    row = pl.BlockSpec((tm, D), lambda i: (i, 0))
    return pl.pallas_call(
        _xattn_kernel,
        grid=(T // tm,),
        in_specs=[row, _const_spec((1, D)), _const_spec((D, D)), _const_spec((1, hd)),
                  pl.BlockSpec((1, D, M), lambda i: (i // per_seq, 0, 0)),
                  pl.BlockSpec((1, X_HEADS * M, D), lambda i: (i // per_seq, 0, 0))],
        out_specs=row,
        out_shape=jax.ShapeDtypeStruct((T, D), F32),
        scratch_shapes=[pltpu.VMEM((tm, X_HEADS * M), BF16)],
        compiler_params=_params("parallel"),
        name="xattn",
    )(x, gain.reshape(1, D), wq, q_norm.reshape(1, hd), kt, vw)


def _even_pre_kernel(x_ref, g_ref, win_ref, cm_ref, sm_ref, qa_ref, wq_ref, kva_ref, wk_ref, wv_ref,
                     seg_ref, gq_ref, gqs_ref, gk_ref, gkr_ref, gkrs_ref,
                     a_ref, qt_ref, k_ref, vt_ref):
    xn = _rms(x_ref[...], g_ref[...]).astype(BF16)
    c = CONV_DIM
    glu = _dot(xn, win_ref[:, 0:2 * c])
    a_ref[...] = glu[:, :c] / (1.0 + jnp.exp(-glu[:, c:]))
    hp = MLA_HEAD_PAD
    z = _dot(xn, win_ref[:, 2 * c:])
    zq = z[:, :Q_LORA]
    zkv = z[:, Q_LORA:Q_LORA + KV_LORA]
    zkr = z[:, Q_LORA + KV_LORA:Q_LORA + KV_LORA + hp]
    zkr_sw = z[:, Q_LORA + KV_LORA + hp:]
    cm = cm_ref[...]
    sm = sm_ref[...]

    rs_kr = lax.rsqrt(jnp.sum(zkr * zkr, axis=-1, keepdims=True) * (1.0 / QK_ROPE) + EPS)
    k_rope = (zkr * rs_kr * gkr_ref[...]) * cm + (zkr_sw * rs_kr * gkrs_ref[...]) * sm

    zq_n = _rms(zq, qa_ref[...]).astype(BF16)
    zkv_n = _rms(zkv, kva_ref[...]).astype(BF16)
    vt_ref[0] = _dot(zkv_n, wv_ref[...]).T.astype(BF16)
    pair = lambda t: jnp.concatenate([t, t], axis=1)
    cm2, sm2, k_rope2 = pair(cm), pair(sm), pair(k_rope)
    pw = 2 * hp
    nq = MLA_HEADS * hp
    seg = seg_ref[...]
    for h in range(MLA_HEADS // 2):
        sl = slice(h * pw, (h + 1) * pw)
        qh = _dot(zq_n, wq_ref[:, sl])
        qh_sw = _dot(zq_n, wq_ref[:, nq + h * pw:nq + (h + 1) * pw])
        rs = lax.rsqrt(_dot((qh * qh).astype(BF16), seg) + EPS)
        qr = (qh * rs * gq_ref[...]) * cm2 + (qh_sw * rs * gqs_ref[...]) * sm2
        qt_ref[0, sl, :] = (qr * MLA_Q_SCALE).T.astype(BF16)
        kh = _dot(zkv_n, wk_ref[:, sl])
        rk = lax.rsqrt(_dot((kh * kh).astype(BF16), seg) + EPS)
        k_ref[:, sl] = (kh * rk * gk_ref[...] + k_rope2).astype(BF16)


def _even_pre(x, gain, w_in_p, cm, sm, qa, wq_p, kva, wk_p, wv_p, seg, gq, gqs, gk, gkr, gkrs, tm):
    T, D = x.shape
    row = lambda w: pl.BlockSpec((tm, w), lambda i: (i, 0))
    hq, hv = MLA_HEADS * MLA_HEAD_PAD, MLA_HEADS * V_HEAD
    consts = [gain.reshape(1, D), w_in_p]
    tables = [cm, sm]
    rest = [qa, wq_p, kva, wk_p, wv_p, seg, gq, gqs, gk, gkr, gkrs]
    return pl.pallas_call(
        _even_pre_kernel,
        grid=(T // tm,),
        in_specs=[row(D)] + [_const_spec(a.shape) for a in consts] + [row(V7X_LANES)] * 2
                 + [_const_spec(a.shape) for a in rest],
        out_specs=[row(CONV_DIM), pl.BlockSpec((1, hq, tm), lambda i: (i, 0, 0)), row(hq),
                   pl.BlockSpec((1, hv, tm), lambda i: (i, 0, 0))],
        out_shape=[jax.ShapeDtypeStruct((T, CONV_DIM), F32), jax.ShapeDtypeStruct((T // tm, hq, tm), BF16),
                   jax.ShapeDtypeStruct((T, hq), BF16), jax.ShapeDtypeStruct((T // tm, hv, tm), BF16)],
        compiler_params=_params("parallel"),
        name="even_pre",
    )(x, *consts, *tables, *rest)


MLA_HEADS_PER_STEP = 8
MLA_Q_SCALE = (QK_NOPE + QK_ROPE) ** -0.5 * math.log2(math.e)


MLA_SUM_ROWS = 16


def _mla_kernel(qt_ref, k_ref, vt_ref, o_ref, m_ref, acc_ref, s_ref):
    qi = pl.program_id(2)
    tq = qt_ref.shape[2]
    hp = MLA_HEAD_PAD
    ones = jnp.ones((MLA_SUM_ROWS, tq), BF16)

    m_ref[...] = jnp.full(m_ref.shape, -jnp.inf, F32)
    acc_ref[...] = jnp.zeros(acc_ref.shape, F32)

    def block(j, masked):
        rows = pl.ds(pl.multiple_of(j * tq, tq), tq)
        for h in range(MLA_HEADS_PER_STEP):
            s_ref[h] = _dot(k_ref[rows, h * hp:(h + 1) * hp], qt_ref[0, h * hp:(h + 1) * hp, :])
        for h in range(MLA_HEADS_PER_STEP):
            s = s_ref[h]
            if masked:
                key = lax.broadcasted_iota(jnp.int32, s.shape, 0)
                qry = lax.broadcasted_iota(jnp.int32, s.shape, 1)
                s = jnp.where(key <= qry, s, -jnp.inf)
            m_old = m_ref[h]
            m_new = jnp.maximum(m_old, jnp.max(s, axis=0, keepdims=True))
            p = jnp.exp2(s - m_new).astype(BF16)
            v1 = jnp.concatenate([vt_ref[j, h * V_HEAD:(h + 1) * V_HEAD, :], ones], axis=0)
            acc_ref[h] = jnp.exp2(m_old - m_new) * acc_ref[h] + _dot(v1, p)
            m_ref[h] = m_new

    def body(j, carry):
        block(j, False)
        return carry

    lax.fori_loop(0, qi, body, 0)
    block(qi, True)
    out = [acc_ref[h, 0:V_HEAD, :] / acc_ref[h, V_HEAD:V_HEAD + 1, :] for h in range(MLA_HEADS_PER_STEP)]
    o_ref[...] = jnp.concatenate(out, axis=0).T.astype(BF16)


def _mla_attention(qt, k, vt, batch, seq, tq):
    T = k.shape[0]
    hs = MLA_HEADS_PER_STEP
    nq = seq // tq
    qw, vw = hs * MLA_HEAD_PAD, hs * V_HEAD
    return pl.pallas_call(
        _mla_kernel,
        grid=(batch, MLA_HEADS // hs, nq),
        in_specs=[pl.BlockSpec((1, qw, tq), lambda b, h, i: (b * nq + i, h, 0)),
                  pl.BlockSpec((seq, qw), lambda b, h, i: (b, h)),
                  pl.BlockSpec((nq, vw, tq), lambda b, h, i: (b, h, 0))],
        out_specs=pl.BlockSpec((tq, vw), lambda b, h, i: (b * nq + i, h)),
        out_shape=jax.ShapeDtypeStruct((T, MLA_HEADS * V_HEAD), BF16),
        scratch_shapes=[pltpu.VMEM((hs, 1, tq), F32), pltpu.VMEM((hs, V_HEAD + MLA_SUM_ROWS, tq), F32),
                        pltpu.VMEM((hs, tq, tq), F32)],
        compiler_params=_params("parallel", "parallel", "arbitrary"),
        name="mla_attention",
    )(qt, k, vt)


CONV_ROWS = 128
V7X_SUBLANES = 8


def _causal_conv(win_ref, cw_ref, ts):
    off = CONV_HALO - (CONV_WIDTH - 1)
    sub = V7X_SUBLANES
    lane_blocks = []
    for lb in range(win_ref.shape[1] // V7X_LANES):
        lanes = slice(lb * V7X_LANES, (lb + 1) * V7X_LANES)
        row_blocks = []
        for base in range(0, ts, CONV_ROWS):
            y = None
            for r in range(sub):
                n = CONV_ROWS + (sub if r else 0)
                z = None
                for q in range(-(-(off + CONV_WIDTH) // sub)):
                    j = sub * q + r - off
                    if 0 <= j < CONV_WIDTH:
                        lo = base + sub * q
                        term = cw_ref[j:j + 1, lanes] * win_ref[lo:lo + n, lanes]
                        z = term if z is None else z + term
                z = z[r:r + CONV_ROWS]
                y = z if y is None else y + z
            row_blocks.append(y)
        lane_blocks.append(jnp.concatenate(row_blocks, axis=0))
    return jnp.concatenate(lane_blocks, axis=1)


def _even_post_kernel(x_ref, halo_ref, a_ref, m_ref, cw_ref, cb_ref, lg_ref, lb_ref, wa_ref, wm_ref,
                      o_ref, win_ref, *, per_seq):
    ts = a_ref.shape[0]
    first = (pl.program_id(0) % per_seq) == 0
    win_ref[0:CONV_HALO, :] = jnp.where(first, 0.0, halo_ref[...])
    win_ref[CONV_HALO:, :] = a_ref[...]
    y = _causal_conv(win_ref, cw_ref, ts) + cb_ref[...]
    mu = jnp.mean(y, axis=-1, keepdims=True)
    yc = y - mu
    var = jnp.mean(yc * yc, axis=-1, keepdims=True)
    act = _silu(yc * lax.rsqrt(var + EPS) * lg_ref[...] + lb_ref[...]).astype(BF16)
    o_ref[...] = x_ref[...] + _dot(act, wa_ref[...]) + _dot(m_ref[...], wm_ref[...])


def _even_post(x, a, m, conv_w, conv_b, ln_g, ln_b, w_out_a, w_out_m, ts, seq):
    T, D = x.shape
    C = a.shape[1]
    per_seq = seq // ts
    halo_blocks = ts // CONV_HALO
    row = lambda w: pl.BlockSpec((ts, w), lambda i: (i, 0))
    return pl.pallas_call(
        functools.partial(_even_post_kernel, per_seq=per_seq),
        grid=(T // ts,),
        in_specs=[row(D),
                  pl.BlockSpec((CONV_HALO, C), lambda i: (jnp.maximum(i * halo_blocks - 1, 0), 0)),
                  row(C), row(C), _const_spec((CONV_WIDTH, C)), _const_spec((1, C)),
                  _const_spec((1, C)), _const_spec((1, C)), _const_spec((C, D)), _const_spec((C, D))],
        out_specs=row(D),
        out_shape=jax.ShapeDtypeStruct((T, D), F32),
        scratch_shapes=[pltpu.VMEM((CONV_HALO + ts, C), F32)],
        compiler_params=_params("parallel"),
        name="even_post",
    )(x, a, a, m, conv_w, conv_b.reshape(1, C), ln_g.reshape(1, C), ln_b.reshape(1, C),
      w_out_a, w_out_m)


def _odd_pre_kernel(x_ref, g_ref, w_ref, cos_ref, sin_ref, q_ref, kt_ref, v_ref, gate_ref):
    xn = _rms(x_ref[...], g_ref[...]).astype(BF16)
    cos = cos_ref[...]
    sin = sin_ref[...]
    half = RET_QK // 2
    nqk = RET_HEADS * RET_QK
    k_scale = RET_QK ** -0.5
    nv = RET_HEADS * RET_V
    for h in range(RET_HEADS):
        sl = slice(h * RET_V, (h + 1) * RET_V)
        gate_ref[:, sl] = _silu(_dot(xn, w_ref[:, 2 * nqk + nv + h * RET_V:2 * nqk + nv + (h + 1) * RET_V]))
    for h in range(RET_HEADS):
        lo = h * RET_QK
        z = _dot(xn, w_ref[:, lo:lo + RET_QK])
        x1, x2 = z[:, :half], z[:, half:]
        q_ref[:, lo:lo + half] = (x1 * cos - x2 * sin).astype(BF16)
        q_ref[:, lo + half:lo + RET_QK] = (x1 * sin + x2 * cos).astype(BF16)
        z = _dot(xn, w_ref[:, nqk + lo:nqk + lo + RET_QK])
        x1, x2 = z[:, :half], z[:, half:]
        k = jnp.concatenate([x1 * cos - x2 * sin, x1 * sin + x2 * cos], axis=1) * k_scale
        kt_ref[0, lo:lo + RET_QK, :] = k.T.astype(BF16)
    for h in range(RET_HEADS):
        sl = slice(h * RET_V, (h + 1) * RET_V)
        v_ref[:, sl] = _dot(xn, w_ref[:, 2 * nqk + h * RET_V:2 * nqk + (h + 1) * RET_V]).astype(BF16)


def _odd_pre(x, gain, w_in, cos_r, sin_r, tm):
    T, D = x.shape
    nqk, nv = RET_HEADS * RET_QK, RET_HEADS * RET_V
    row = lambda w: pl.BlockSpec((tm, w), lambda i: (i, 0))
    return pl.pallas_call(
        _odd_pre_kernel,
        grid=(T // tm,),
        in_specs=[row(D), _const_spec((1, D)), _const_spec(w_in.shape), row(V7X_LANES), row(V7X_LANES)],
        out_specs=[row(nqk), pl.BlockSpec((1, nqk, tm), lambda i: (i, 0, 0)), row(nv), row(nv)],
        out_shape=[jax.ShapeDtypeStruct((T, nqk), BF16), jax.ShapeDtypeStruct((T // tm, nqk, tm), BF16),
                   jax.ShapeDtypeStruct((T, nv), BF16), jax.ShapeDtypeStruct((T, nv), F32)],
        compiler_params=_params("parallel"),
        name="odd_pre",
    )(x, gain.reshape(1, D), w_in, cos_r, sin_r)


def _retention_kernel(x_ref, q_ref, kt_ref, v_ref, gate_ref, decay_ref, xi_ref, zeta_ref, gch_ref,
                      gng_ref, gnb_ref, wo_ref, o_ref, state_ref, y_ref):
    @pl.when(pl.program_id(1) == 0)
    def _():
        state_ref[...] = jnp.zeros(state_ref.shape, F32)

    ts = x_ref.shape[0]
    C = RET_BLOCK
    for c in range(ts // C):
        rows = slice(c * C, (c + 1) * C)
        for h in range(RET_HEADS):
            qk = slice(h * RET_QK, (h + 1) * RET_QK)
            vv = slice(h * RET_V, (h + 1) * RET_V)
            qc = q_ref[rows, qk]
            kt = kt_ref[0, qk, rows]
            vc = v_ref[rows, vv]
            state = state_ref[h]
            scores = (_dot(qc, kt) * decay_ref[h]).astype(BF16)
            out = _dot(scores, vc) + _dot(qc, state.astype(BF16)) * xi_ref[h]
            kz = (kt.astype(F32) * zeta_ref[h]).astype(BF16)
            state_ref[h] = state * gch_ref[h] + _dot(kz, vc)
            mu = jnp.mean(out, axis=-1, keepdims=True)
            oc = out - mu
            var = jnp.mean(oc * oc, axis=-1, keepdims=True)
            gn = oc * lax.rsqrt(var + EPS) * gng_ref[:, vv] + gnb_ref[:, vv]
            y_ref[rows, vv] = (gate_ref[rows, vv] * gn).astype(BF16)
    o_ref[...] = x_ref[...] + _dot(y_ref[...], wo_ref[...])


def _retention(x, q, k, v, gate, decay, xi, zeta, gch, gn_g, gn_b, w_out, batch, seq, ts):
    T, D = x.shape
    nqk, nv = RET_HEADS * RET_QK, RET_HEADS * RET_V
    per_seq = seq // ts
    row = lambda w: pl.BlockSpec((ts, w), lambda b, i: (b * per_seq + i, 0))
    return pl.pallas_call(
        _retention_kernel,
        grid=(batch, per_seq),
        in_specs=[row(D), row(nqk), pl.BlockSpec((1, nqk, ts), lambda b, i: (b * per_seq + i, 0, 0)),
                  row(nv), row(nv), _const_spec(decay.shape), _const_spec(xi.shape), _const_spec(zeta.shape),
                  _const_spec(gch.shape), _const_spec((1, nv)), _const_spec((1, nv)),
                  _const_spec((nv, D))],
        out_specs=row(D),
        out_shape=jax.ShapeDtypeStruct((T, D), F32),
        scratch_shapes=[pltpu.VMEM((RET_HEADS, RET_QK, RET_V), F32), pltpu.VMEM((ts, nv), BF16)],
        compiler_params=_params("parallel", "arbitrary"),
        name="retention",
    )(x, q, k, v, gate, decay, xi, zeta, gch, gn_g.reshape(1, nv), gn_b.reshape(1, nv), w_out)


def _retention_constants():
    H, C = RET_HEADS, RET_BLOCK
    log_g = jnp.log1p(-jnp.exp2(-5.0 - jnp.arange(H, dtype=F32)))
    idx = jnp.arange(C, dtype=F32)
    diff = idx[:, None] - idx[None, :]
    decay = jnp.where(diff >= 0, jnp.exp(log_g[:, None, None] * jnp.maximum(diff, 0.0)), 0.0)
    xi = jnp.exp(log_g[:, None] * (idx + 1.0))[:, :, None]
    zeta = jnp.exp(log_g[:, None] * (C - 1.0 - idx))[:, None, :]
    gch = jnp.exp(log_g * C)[:, None, None]
    return decay, xi, zeta, gch


def _swap_halves(w):
    half = w.shape[-1] // 2
    return jnp.concatenate([w[..., half:], w[..., :half]], axis=-1)


def _mla_weights(w_in, w_q_b, w_kv_b, q_nope_norm, k_nope_norm, q_rope_norm, k_rope_norm):
    D = w_in.shape[0]
    pad = MLA_HEAD_PAD - QK_NOPE - QK_ROPE
    z = lambda *s: jnp.zeros(s, F32)
    base = 2 * CONV_DIM + Q_LORA + KV_LORA
    w_kr = w_in[:, base:base + QK_ROPE]
    grp = lambda w: jnp.concatenate([z(D, QK_NOPE), w, z(D, pad)], axis=1)
    w_in_p = jnp.concatenate([w_in[:, :base], grp(w_kr), grp(_swap_halves(w_kr))], axis=1)

    wq = w_q_b.reshape(Q_LORA, MLA_HEADS, QK_NOPE + QK_ROPE)
    wq_nope, wq_rope = wq[..., :QK_NOPE], wq[..., QK_NOPE:]
    zq = z(Q_LORA, MLA_HEADS, pad)
    wq_p = jnp.concatenate([wq_nope, wq_rope, zq], axis=-1).reshape(Q_LORA, -1)
    wq_sw = jnp.concatenate([jnp.zeros_like(wq_nope), _swap_halves(wq_rope), zq], axis=-1).reshape(Q_LORA, -1)
    wq_all = jnp.concatenate([wq_p, wq_sw], axis=1)

    wkv = w_kv_b.reshape(KV_LORA, MLA_HEADS, QK_NOPE + V_HEAD)
    wk_p = jnp.concatenate([wkv[..., :QK_NOPE], z(KV_LORA, MLA_HEADS, MLA_HEAD_PAD - QK_NOPE)],
                           axis=-1).reshape(KV_LORA, -1)
    wv_p = wkv[..., QK_NOPE:].reshape(KV_LORA, -1)

    lane = jnp.arange(MLA_HEAD_PAD)
    in_nope = lane < QK_NOPE
    in_rope = (lane >= QK_NOPE) & (lane < QK_NOPE + QK_ROPE)
    seg = (jnp.where(in_nope[:, None] & in_nope[None, :], 1.0 / QK_NOPE, 0.0)
           + jnp.where(in_rope[:, None] & in_rope[None, :], 1.0 / QK_ROPE, 0.0))
    vec = lambda nope, rope: jnp.concatenate([nope, rope, z(pad)]).reshape(1, MLA_HEAD_PAD)
    pair = lambda v: jnp.concatenate([v, v], axis=1)
    gq = pair(vec(q_nope_norm, q_rope_norm))
    gqs = pair(vec(jnp.zeros_like(q_nope_norm), _swap_halves(q_rope_norm)))
    gk = pair(vec(k_nope_norm, z(QK_ROPE)))
    gkr = vec(z(QK_NOPE), k_rope_norm)
    gkrs = vec(z(QK_NOPE), _swap_halves(k_rope_norm))
    zs = jnp.zeros_like(seg)
    seg2 = jnp.concatenate([jnp.concatenate([seg, zs], axis=1), jnp.concatenate([zs, seg], axis=1)], axis=0)
    return (w_in_p.astype(BF16), wq_all.astype(BF16), wk_p.astype(BF16), wv_p.astype(BF16),
            seg2.astype(BF16), gq, gqs, gk, gkr, gkrs)


def kernel(x, mem, positions, ffn1_norm, ffn1_w_gate, ffn1_w_up, ffn1_w_down, ffn2_norm, ffn2_w_gate, ffn2_w_up, ffn2_w_down, mix_norm, xattn_norm, mem_norm, xattn_wq, xattn_wk, xattn_wv, xattn_wo, xattn_q_norm, xattn_k_norm, ev_w_in, ev_conv_w, ev_conv_b, ev_conv_ln_g, ev_conv_ln_b, ev_q_a_norm, ev_w_q_b, ev_kv_a_norm, ev_w_kv_b, ev_q_nope_norm, ev_k_nope_norm, ev_q_rope_norm, ev_k_rope_norm, ev_w_out, od_w_in, od_gn_g, od_gn_b, od_w_out):
    B, S, D = x.shape
    depth = ffn1_norm.shape[0]
    tm = min(TOKEN_TILE, S)
    tf = min(FFN_TOKEN_TILE, B * S)
    assert S % tm == 0 and (B * S) % tf == 0 and S % min(tf, S) == 0 and tm % RET_BLOCK == 0 and tm % CONV_HALO == 0
    bf = lambda w: w.astype(BF16)

    h = x.reshape(B * S, D)
    cos_r, sin_r, cm, sm = _rope_tables(positions, tm)
    decay, xi, zeta, gch = _retention_constants()

    for l in range(depth):
        h = _ffn(h, ffn1_norm[l], bf(ffn1_w_gate[l]), bf(ffn1_w_up[l]), bf(ffn1_w_down[l]), tf)
        if l % 2 == 0:
            e = l // 2
            (w_in_p, wq_p, wk_p, wv_p, seg, gq, gqs, gk, gkr, gkrs) = _mla_weights(
                ev_w_in[e], ev_w_q_b[e], ev_w_kv_b[e], ev_q_nope_norm[e], ev_k_nope_norm[e],
                ev_q_rope_norm[e], ev_k_rope_norm[e])
            a, q, k, v = _even_pre(h, mix_norm[l], w_in_p, cm, sm, ev_q_a_norm[e].reshape(1, -1), wq_p,
                                   ev_kv_a_norm[e].reshape(1, -1), wk_p, wv_p, seg, gq, gqs, gk, gkr,
                                   gkrs, tm)
            m = _mla_attention(q, k, v, B, S, tm)
            h = _even_post(h, a, m, ev_conv_w[e], ev_conv_b[e], ev_conv_ln_g[e], ev_conv_ln_b[e],
                           bf(ev_w_out[e][:CONV_DIM]), bf(ev_w_out[e][CONV_DIM:]), tm, S)
        else:
            o = l // 2
            q, k, v, gate = _odd_pre(h, mix_norm[l], bf(od_w_in[o]), cos_r, sin_r, tm)
            h = _retention(h, q, k, v, gate, decay, xi, zeta, gch, od_gn_g[o], od_gn_b[o],
                           bf(od_w_out[o]), B, S, tm)
        kt, vw = _memkv(mem, mem_norm[l], bf(xattn_wk[l]), bf(xattn_wv[l]), bf(xattn_wo[l]), xattn_k_norm[l])
        h = _xattn(h, xattn_norm[l], bf(xattn_wq[l]), xattn_q_norm[l], kt, vw, min(tf, S), S)
        h = _ffn(h, ffn2_norm[l], bf(ffn2_w_gate[l]), bf(ffn2_w_up[l]), bf(ffn2_w_down[l]), tf)
    return h.reshape(B, S, D)
```

```python
import functools
import math

import jax
import jax.numpy as jnp
from jax import lax
from jax.experimental import pallas as pl
from jax.experimental.pallas import tpu as pltpu

F32 = jnp.float32
BF16 = jnp.bfloat16

EPS = 1e-6
ROPE_THETA = 10000.0

D_FF = 2816
CONV_DIM = 512
CONV_WIDTH = 31
MLA_HEADS = 8
Q_LORA = 256
KV_LORA = 128
QK_NOPE = 64
QK_ROPE = 32
V_HEAD = 64
RET_HEADS = 4
RET_QK = 256
RET_V = 512
X_HEADS = 4

V7X_LANES = 128
V7X_MXU_DIM = 256
V7X_VMEM_LIMIT_BYTES = 56 * 1024 * 1024

TOKEN_TILE = 512
FFN_TOKEN_TILE = 1024
RET_BLOCK = 256
CONV_HALO = 32
MLA_HEAD_PAD = 128


def _params(*sem):
    return pltpu.CompilerParams(dimension_semantics=sem, vmem_limit_bytes=V7X_VMEM_LIMIT_BYTES)


def _const_spec(shape):
    nd = len(shape)
    return pl.BlockSpec(shape, lambda *_: (0,) * nd, pipeline_mode=pl.Buffered(1))


def _rms(x, gain):
    return x * lax.rsqrt(jnp.mean(x * x, axis=-1, keepdims=True) + EPS) * gain


def _silu(x):
    return x / (1.0 + jnp.exp(-x))


def _dot(a, b):
    return jnp.dot(a, b, preferred_element_type=F32)


def _dot_nt(a, b):
    return lax.dot_general(a, b, (((1,), (1,)), ((), ())), preferred_element_type=F32)


def _dot_tn(a, b):
    return lax.dot_general(a, b, (((0,), (0,)), ((), ())), preferred_element_type=F32)


def _select_lanes(t, sel):
    hi = t.astype(BF16)
    rest = t - hi.astype(F32)
    mid = rest.astype(BF16)
    lo = (rest - mid.astype(F32)).astype(BF16)
    return _dot(hi, sel) + _dot(mid, sel) + _dot(lo, sel)


def _rope_kernel(pos_ref, inv_ref, selc_ref, sels_ref, one_ref, cos_ref, sin_ref, cm_ref, sm_ref):
    ang = pos_ref[...].astype(F32) * inv_ref[...]
    cos = jnp.cos(ang)
    sin = jnp.sin(ang)
    cos_ref[...] = cos
    sin_ref[...] = sin
    cm_ref[...] = _select_lanes(cos, selc_ref[...]) + one_ref[...]
    sm_ref[...] = _select_lanes(sin, sels_ref[...])


def _rope_tables(positions, tm):
    T = positions.size
    inv = (ROPE_THETA ** (-jnp.arange(0, RET_QK, 2, dtype=F32) / RET_QK)).reshape(1, RET_QK // 2)
    assert RET_QK % QK_ROPE == 0 and RET_QK // 2 == V7X_LANES
    half = QK_ROPE // 2
    src = jnp.arange(half) * (RET_QK // QK_ROPE)
    lane = jnp.arange(V7X_LANES)
    hit = lambda dst0: (lane[:, None] == src[None, :]).astype(F32) @ \
        (jnp.arange(half)[:, None] + dst0 == lane[None, :]).astype(F32)
    first, second = hit(QK_NOPE), hit(QK_NOPE + half)
    sel_cos = (first + second).astype(BF16)
    sel_sin = (second - first).astype(BF16)
    one = ((lane < QK_NOPE) | (lane >= QK_NOPE + QK_ROPE)).astype(F32).reshape(1, V7X_LANES)
    row = pl.BlockSpec((tm, V7X_LANES), lambda i: (i, 0))
    mat = _const_spec((V7X_LANES, V7X_LANES))
    out = jax.ShapeDtypeStruct((T, V7X_LANES), F32)
    return pl.pallas_call(
        _rope_kernel,
        grid=(T // tm,),
        in_specs=[pl.BlockSpec((tm, 1), lambda i: (i, 0)), _const_spec((1, V7X_LANES)), mat, mat,
                  _const_spec((1, V7X_LANES))],
        out_specs=[row, row, row, row],
        out_shape=[out, out, out, out],
        compiler_params=_params("parallel"),
        name="rope_tables",
    )(positions.reshape(T, 1), inv, sel_cos, sel_sin, one)


def _ff_chunks(d_ff, width):
    edges = list(range(0, d_ff, width)) + [d_ff]
    return [(a, b) for a, b in zip(edges[:-1], edges[1:])]


def _ffn_kernel(x_ref, g_ref, wg_ref, wu_ref, wd_ref, o_ref, h_ref):
    x = x_ref[...]
    xn = _rms(x, g_ref[...]).astype(BF16)
    for a, b in _ff_chunks(h_ref.shape[1], 2 * V7X_MXU_DIM):
        gate = _dot(xn, wg_ref[:, a:b])
        up = _dot(xn, wu_ref[:, a:b])
        h_ref[:, a:b] = (_silu(gate) * up).astype(BF16)
    o_ref[...] = x + 0.5 * _dot(h_ref[...], wd_ref[...])


def _ffn(x, gain, w_gate, w_up, w_down, tm):
    T, D = x.shape
    d_ff = w_gate.shape[1]
    row = pl.BlockSpec((tm, D), lambda i: (i, 0))
    return pl.pallas_call(
        _ffn_kernel,
        grid=(T // tm,),
        in_specs=[row, _const_spec((1, D)), _const_spec((D, d_ff)), _const_spec((D, d_ff)),
                  _const_spec((d_ff, D))],
        out_specs=row,
        out_shape=jax.ShapeDtypeStruct((T, D), F32),
        scratch_shapes=[pltpu.VMEM((tm, d_ff), BF16)],
        compiler_params=_params("parallel"),
        name="ffn",
    )(x, gain.reshape(1, D), w_gate, w_up, w_down)


def _memkv_kernel(mem_ref, g_ref, wk_ref, wv_ref, wo_ref, kn_ref, kt_ref, vw_ref):
    M = mem_ref.shape[0]
    mn = _rms(mem_ref[...], g_ref[...]).astype(BF16)
    k = _dot(mn, wk_ref[...])
    v = _dot(mn, wv_ref[...]).astype(BF16)
    hd = kn_ref.shape[1]
    for h in range(X_HEADS):
        sl = slice(h * hd, (h + 1) * hd)
        kh = _rms(k[:, sl], kn_ref[...])
        kt_ref[0, sl, :] = kh.T.astype(BF16)
        vw_ref[0, h * M:(h + 1) * M, :] = _dot(v[:, sl], wo_ref[sl, :]).astype(BF16)


def _memkv(mem, gain, wk, wv, wo, k_norm):
    B, M, D = mem.shape
    hd = D // X_HEADS
    return pl.pallas_call(
        _memkv_kernel,
        grid=(B,),
        in_specs=[pl.BlockSpec((M, D), lambda b: (b, 0)), _const_spec((1, D)),
                  _const_spec((D, D)), _const_spec((D, D)), _const_spec((D, D)), _const_spec((1, hd))],
        out_specs=[pl.BlockSpec((1, D, M), lambda b: (b, 0, 0)),
                   pl.BlockSpec((1, X_HEADS * M, D), lambda b: (b, 0, 0))],
        out_shape=[jax.ShapeDtypeStruct((B, D, M), BF16), jax.ShapeDtypeStruct((B, X_HEADS * M, D), BF16)],
        compiler_params=_params("parallel"),
        name="xattn_memkv",
    )(mem.reshape(B * M, D), gain.reshape(1, D), wk, wv, wo, k_norm.reshape(1, hd))


def _xattn_kernel(x_ref, g_ref, wq_ref, qn_ref, kt_ref, vw_ref, o_ref, p_ref):
    x = x_ref[...]
    xn = _rms(x, g_ref[...]).astype(BF16)
    q = _dot(xn, wq_ref[...])
    hd = qn_ref.shape[1]
    M = kt_ref.shape[2]
    scale = hd ** -0.5
    for h in range(X_HEADS):
        sl = slice(h * hd, (h + 1) * hd)
        qh = (_rms(q[:, sl], qn_ref[...]) * scale).astype(BF16)
        s = _dot(qh, kt_ref[0, sl, :])
        p = jnp.exp(s - jnp.max(s, axis=-1, keepdims=True))
        p_ref[:, h * M:(h + 1) * M] = (p * (1.0 / jnp.sum(p, axis=-1, keepdims=True))).astype(BF16)
    o_ref[...] = x + _dot(p_ref[...], vw_ref[0])


def _xattn(x, gain, wq, q_norm, kt, vw, tm, seq):
    T, D = x.shape
    M = kt.shape[2]
    hd = D // X_HEADS
    per_seq = seq // tm
    row = pl.BlockSpec((tm, D), lambda i: (i, 0))
    return pl.pallas_call(
        _xattn_kernel,
        grid=(T // tm,),
        in_specs=[row, _const_spec((1, D)), _const_spec((D, D)), _const_spec((1, hd)),
                  pl.BlockSpec((1, D, M), lambda i: (i // per_seq, 0, 0)),
                  pl.BlockSpec((1, X_HEADS * M, D), lambda i: (i // per_seq, 0, 0))],
        out_specs=row,
        out_shape=jax.ShapeDtypeStruct((T, D), F32),
        scratch_shapes=[pltpu.VMEM((tm, X_HEADS * M), BF16)],
        compiler_params=_params("parallel"),
        name="xattn",
    )(x, gain.reshape(1, D), wq, q_norm.reshape(1, hd), kt, vw)


def _even_pre_kernel(x_ref, g_ref, win_ref, cm_ref, sm_ref, qa_ref, wq_ref, kva_ref, wk_ref, wv_ref,
                     seg_ref, gq_ref, gqs_ref, gk_ref, gkr_ref, gkrs_ref,
                     a_ref, qt_ref, k_ref, vt_ref):
    xn = _rms(x_ref[...], g_ref[...]).astype(BF16)
    c = CONV_DIM
    glu = _dot(xn, win_ref[:, 0:2 * c])
    a_ref[...] = glu[:, :c] / (1.0 + jnp.exp(-glu[:, c:]))
    hp = MLA_HEAD_PAD
    z = _dot(xn, win_ref[:, 2 * c:])
    zq = z[:, :Q_LORA]
    zkv = z[:, Q_LORA:Q_LORA + KV_LORA]
    zkr = z[:, Q_LORA + KV_LORA:Q_LORA + KV_LORA + hp]
    zkr_sw = z[:, Q_LORA + KV_LORA + hp:]
    cm = cm_ref[...]
    sm = sm_ref[...]

    rs_kr = lax.rsqrt(jnp.sum(zkr * zkr, axis=-1, keepdims=True) * (1.0 / QK_ROPE) + EPS)
    k_rope = (zkr * rs_kr * gkr_ref[...]) * cm + (zkr_sw * rs_kr * gkrs_ref[...]) * sm

    zq_n = _rms(zq, qa_ref[...]).astype(BF16)
    zkv_n = _rms(zkv, kva_ref[...]).astype(BF16)
    tq = qt_ref.shape[2]
    blocks = [slice(r * tq, (r + 1) * tq) for r in range(qt_ref.shape[0])]
    v = _dot(zkv_n, wv_ref[...])
    for r, rows in enumerate(blocks):
        vt_ref[r] = v[rows].T.astype(BF16)
    pair = lambda t: jnp.concatenate([t, t], axis=1)
    k_rope2 = pair(k_rope)
    q_cos = pair(cm) * (gq_ref[...] * MLA_Q_SCALE)
    q_sin = pair(sm) * (gqs_ref[...] * MLA_Q_SCALE)
    pw = 2 * hp
    nq = MLA_HEADS * hp
    seg = seg_ref[...]
    for h in range(MLA_HEADS // 2):
        sl = slice(h * pw, (h + 1) * pw)
        qh = _dot(zq_n, wq_ref[:, sl])
        qh_sw = _dot(zq_n, wq_ref[:, nq + h * pw:nq + (h + 1) * pw])
        rs = lax.rsqrt(_dot((qh * qh).astype(BF16), seg) + EPS)
        qr = rs * (qh * q_cos + qh_sw * q_sin)
        for r, rows in enumerate(blocks):
            qt_ref[r, sl, :] = qr[rows].T.astype(BF16)
        kh = _dot(zkv_n, wk_ref[:, sl])
        rk = lax.rsqrt(_dot((kh * kh).astype(BF16), seg) + EPS)
        k_ref[:, sl] = (kh * rk * gk_ref[...] + k_rope2).astype(BF16)


def _even_pre(x, gain, w_in_p, cm, sm, qa, wq_p, kva, wk_p, wv_p, seg, gq, gqs, gk, gkr, gkrs, tm, tq):
    T, D = x.shape
    row = lambda w: pl.BlockSpec((tm, w), lambda i: (i, 0))
    hq, hv = MLA_HEADS * MLA_HEAD_PAD, MLA_HEADS * V_HEAD
    consts = [gain.reshape(1, D), w_in_p]
    tables = [cm, sm]
    rest = [qa, wq_p, kva, wk_p, wv_p, seg, gq, gqs, gk, gkr, gkrs]
    return pl.pallas_call(
        _even_pre_kernel,
        grid=(T // tm,),
        in_specs=[row(D)] + [_const_spec(a.shape) for a in consts] + [row(V7X_LANES)] * 2
                 + [_const_spec(a.shape) for a in rest],
        out_specs=[row(CONV_DIM), pl.BlockSpec((tm // tq, hq, tq), lambda i: (i, 0, 0)), row(hq),
                   pl.BlockSpec((tm // tq, hv, tq), lambda i: (i, 0, 0))],
        out_shape=[jax.ShapeDtypeStruct((T, CONV_DIM), F32), jax.ShapeDtypeStruct((T // tq, hq, tq), BF16),
                   jax.ShapeDtypeStruct((T, hq), BF16), jax.ShapeDtypeStruct((T // tq, hv, tq), BF16)],
        compiler_params=_params("parallel"),
        name="even_pre",
    )(x, *consts, *tables, *rest)


MLA_HEADS_PER_STEP = 8
MLA_Q_SCALE = (QK_NOPE + QK_ROPE) ** -0.5 * math.log2(math.e)


MLA_SUM_ROWS = 16


def _mla_kernel(qt_ref, k_ref, vt_ref, o_ref, m_ref, acc_ref, s_ref):
    qi = pl.program_id(2)
    tq = qt_ref.shape[2]
    hp = MLA_HEAD_PAD
    ones = jnp.ones((MLA_SUM_ROWS, tq), BF16)

    m_ref[...] = jnp.full(m_ref.shape, -jnp.inf, F32)
    acc_ref[...] = jnp.zeros(acc_ref.shape, F32)

    def block(j, masked):
        rows = pl.ds(pl.multiple_of(j * tq, tq), tq)
        for h in range(MLA_HEADS_PER_STEP):
            s_ref[h] = _dot(k_ref[rows, h * hp:(h + 1) * hp], qt_ref[0, h * hp:(h + 1) * hp, :])
        for h in range(MLA_HEADS_PER_STEP):
            s = s_ref[h]
            if masked:
                key = lax.broadcasted_iota(jnp.int32, s.shape, 0)
                qry = lax.broadcasted_iota(jnp.int32, s.shape, 1)
                s = jnp.where(key <= qry, s, -jnp.inf)
            m_old = m_ref[h]
            m_new = jnp.maximum(m_old, jnp.max(s, axis=0, keepdims=True))
            p = jnp.exp2(s - m_new).astype(BF16)
            v1 = jnp.concatenate([vt_ref[j, h * V_HEAD:(h + 1) * V_HEAD, :], ones], axis=0)
            acc_ref[h] = jnp.exp2(m_old - m_new) * acc_ref[h] + _dot(v1, p)
            m_ref[h] = m_new

    def body(j, carry):
        block(j, False)
        return carry

    lax.fori_loop(0, qi, body, 0)
    block(qi, True)
    out = [acc_ref[h, 0:V_HEAD, :] / acc_ref[h, V_HEAD:V_HEAD + 1, :] for h in range(MLA_HEADS_PER_STEP)]
    o_ref[...] = jnp.concatenate(out, axis=0).T.astype(BF16)


def _mla_attention(qt, k, vt, batch, seq, tq):
    T = k.shape[0]
    hs = MLA_HEADS_PER_STEP
    nq = seq // tq
    qw, vw = hs * MLA_HEAD_PAD, hs * V_HEAD
    return pl.pallas_call(
        _mla_kernel,
        grid=(batch, MLA_HEADS // hs, nq),
        in_specs=[pl.BlockSpec((1, qw, tq), lambda b, h, i: (b * nq + i, h, 0)),
                  pl.BlockSpec((seq, qw), lambda b, h, i: (b, h)),
                  pl.BlockSpec((nq, vw, tq), lambda b, h, i: (b, h, 0))],
        out_specs=pl.BlockSpec((tq, vw), lambda b, h, i: (b * nq + i, h)),
        out_shape=jax.ShapeDtypeStruct((T, MLA_HEADS * V_HEAD), BF16),
        scratch_shapes=[pltpu.VMEM((hs, 1, tq), F32), pltpu.VMEM((hs, V_HEAD + MLA_SUM_ROWS, tq), F32),
                        pltpu.VMEM((hs, tq, tq), F32)],
        compiler_params=_params("parallel", "parallel", "arbitrary"),
        name="mla_attention",
    )(qt, k, vt)


CONV_ROWS = 128
V7X_SUBLANES = 8


def _causal_conv(win_ref, cw_ref, ts):
    off = CONV_HALO - (CONV_WIDTH - 1)
    sub = V7X_SUBLANES
    lane_blocks = []
    for lb in range(win_ref.shape[1] // V7X_LANES):
        lanes = slice(lb * V7X_LANES, (lb + 1) * V7X_LANES)
        row_blocks = []
        for base in range(0, ts, CONV_ROWS):
            y = None
            for r in range(sub):
                n = CONV_ROWS + (sub if r else 0)
                z = None
                for q in range(-(-(off + CONV_WIDTH) // sub)):
                    j = sub * q + r - off
                    if 0 <= j < CONV_WIDTH:
                        lo = base + sub * q
                        term = cw_ref[j:j + 1, lanes] * win_ref[lo:lo + n, lanes]
                        z = term if z is None else z + term
                z = z[r:r + CONV_ROWS]
                y = z if y is None else y + z
            row_blocks.append(y)
        lane_blocks.append(jnp.concatenate(row_blocks, axis=0))
    return jnp.concatenate(lane_blocks, axis=1)


def _even_post_kernel(x_ref, halo_ref, a_ref, m_ref, cw_ref, cb_ref, lg_ref, lb_ref, wa_ref, wm_ref,
                      o_ref, win_ref, *, per_seq):
    ts = a_ref.shape[0]
    first = (pl.program_id(0) % per_seq) == 0
    win_ref[0:CONV_HALO, :] = jnp.where(first, 0.0, halo_ref[...])
    win_ref[CONV_HALO:, :] = a_ref[...]
    y = _causal_conv(win_ref, cw_ref, ts) + cb_ref[...]
    mu = jnp.mean(y, axis=-1, keepdims=True)
    yc = y - mu
    var = jnp.mean(yc * yc, axis=-1, keepdims=True)
    act = _silu(yc * lax.rsqrt(var + EPS) * lg_ref[...] + lb_ref[...]).astype(BF16)
    o_ref[...] = x_ref[...] + _dot(act, wa_ref[...]) + _dot(m_ref[...], wm_ref[...])


def _even_post(x, a, m, conv_w, conv_b, ln_g, ln_b, w_out_a, w_out_m, ts, seq):
    T, D = x.shape
    C = a.shape[1]
    per_seq = seq // ts
    halo_blocks = ts // CONV_HALO
    row = lambda w: pl.BlockSpec((ts, w), lambda i: (i, 0))
    return pl.pallas_call(
        functools.partial(_even_post_kernel, per_seq=per_seq),
        grid=(T // ts,),
        in_specs=[row(D),
                  pl.BlockSpec((CONV_HALO, C), lambda i: (jnp.maximum(i * halo_blocks - 1, 0), 0)),
                  row(C), row(C), _const_spec((CONV_WIDTH, C)), _const_spec((1, C)),
                  _const_spec((1, C)), _const_spec((1, C)), _const_spec((C, D)), _const_spec((C, D))],
        out_specs=row(D),
        out_shape=jax.ShapeDtypeStruct((T, D), F32),
        scratch_shapes=[pltpu.VMEM((CONV_HALO + ts, C), F32)],
        compiler_params=_params("parallel"),
        name="even_post",
    )(x, a, a, m, conv_w, conv_b.reshape(1, C), ln_g.reshape(1, C), ln_b.reshape(1, C),
      w_out_a, w_out_m)


def _odd_pre_kernel(x_ref, g_ref, w_ref, cos_ref, sin_ref, q_ref, kt_ref, v_ref, gate_ref):
    xn = _rms(x_ref[...], g_ref[...]).astype(BF16)
    cos = cos_ref[...]
    sin = sin_ref[...]
    half = RET_QK // 2
    nqk = RET_HEADS * RET_QK
    k_scale = RET_QK ** -0.5
    nv = RET_HEADS * RET_V
    for h in range(RET_HEADS):
        sl = slice(h * RET_V, (h + 1) * RET_V)
        gate_ref[:, sl] = _silu(_dot(xn, w_ref[:, 2 * nqk + nv + h * RET_V:2 * nqk + nv + (h + 1) * RET_V]))
    for h in range(RET_HEADS):
        lo = h * RET_QK
        z = _dot(xn, w_ref[:, lo:lo + RET_QK])
        x1, x2 = z[:, :half], z[:, half:]
        q_ref[:, lo:lo + half] = (x1 * cos - x2 * sin).astype(BF16)
        q_ref[:, lo + half:lo + RET_QK] = (x1 * sin + x2 * cos).astype(BF16)
        z = _dot(xn, w_ref[:, nqk + lo:nqk + lo + RET_QK])
        x1, x2 = z[:, :half], z[:, half:]
        k = jnp.concatenate([x1 * cos - x2 * sin, x1 * sin + x2 * cos], axis=1) * k_scale
        kt_ref[0, lo:lo + RET_QK, :] = k.T.astype(BF16)
    for h in range(RET_HEADS):
        sl = slice(h * RET_V, (h + 1) * RET_V)
        v_ref[:, sl] = _dot(xn, w_ref[:, 2 * nqk + h * RET_V:2 * nqk + (h + 1) * RET_V]).astype(BF16)


def _odd_pre(x, gain, w_in, cos_r, sin_r, tm):
    T, D = x.shape
    nqk, nv = RET_HEADS * RET_QK, RET_HEADS * RET_V
    row = lambda w: pl.BlockSpec((tm, w), lambda i: (i, 0))
    return pl.pallas_call(
        _odd_pre_kernel,
        grid=(T // tm,),
        in_specs=[row(D), _const_spec((1, D)), _const_spec(w_in.shape), row(V7X_LANES), row(V7X_LANES)],
        out_specs=[row(nqk), pl.BlockSpec((1, nqk, tm), lambda i: (i, 0, 0)), row(nv), row(nv)],
        out_shape=[jax.ShapeDtypeStruct((T, nqk), BF16), jax.ShapeDtypeStruct((T // tm, nqk, tm), BF16),
                   jax.ShapeDtypeStruct((T, nv), BF16), jax.ShapeDtypeStruct((T, nv), F32)],
        compiler_params=_params("parallel"),
        name="odd_pre",
    )(x, gain.reshape(1, D), w_in, cos_r, sin_r)


def _retention_kernel(x_ref, q_ref, kt_ref, v_ref, gate_ref, decay_ref, xi_ref, zeta_ref, gch_ref,
                      gng_ref, gnb_ref, wo_ref, o_ref, state_ref, y_ref):
    @pl.when(pl.program_id(1) == 0)
    def _():
        state_ref[...] = jnp.zeros(state_ref.shape, F32)

    ts = x_ref.shape[0]
    C = RET_BLOCK
    for c in range(ts // C):
        rows = slice(c * C, (c + 1) * C)
        for h in range(RET_HEADS):
            qk = slice(h * RET_QK, (h + 1) * RET_QK)
            vv = slice(h * RET_V, (h + 1) * RET_V)
            qc = q_ref[rows, qk]
            kt = kt_ref[0, qk, rows]
            vc = v_ref[rows, vv]
            state = state_ref[h]
            scores = (_dot(qc, kt) * decay_ref[h]).astype(BF16)
            out = _dot(scores, vc) + _dot(qc, state.astype(BF16)) * xi_ref[h]
            kz = (kt.astype(F32) * zeta_ref[h]).astype(BF16)
            state_ref[h] = state * gch_ref[h] + _dot(kz, vc)
            mu = jnp.mean(out, axis=-1, keepdims=True)
            oc = out - mu
            var = jnp.mean(oc * oc, axis=-1, keepdims=True)
            gn = oc * lax.rsqrt(var + EPS) * gng_ref[:, vv] + gnb_ref[:, vv]
            y_ref[rows, vv] = (gate_ref[rows, vv] * gn).astype(BF16)
    o_ref[...] = x_ref[...] + _dot(y_ref[...], wo_ref[...])


def _retention(x, q, k, v, gate, decay, xi, zeta, gch, gn_g, gn_b, w_out, batch, seq, ts):
    T, D = x.shape
    nqk, nv = RET_HEADS * RET_QK, RET_HEADS * RET_V
    per_seq = seq // ts
    row = lambda w: pl.BlockSpec((ts, w), lambda b, i: (b * per_seq + i, 0))
    return pl.pallas_call(
        _retention_kernel,
        grid=(batch, per_seq),
        in_specs=[row(D), row(nqk), pl.BlockSpec((1, nqk, ts), lambda b, i: (b * per_seq + i, 0, 0)),
                  row(nv), row(nv), _const_spec(decay.shape), _const_spec(xi.shape), _const_spec(zeta.shape),
                  _const_spec(gch.shape), _const_spec((1, nv)), _const_spec((1, nv)),
                  _const_spec((nv, D))],
        out_specs=row(D),
        out_shape=jax.ShapeDtypeStruct((T, D), F32),
        scratch_shapes=[pltpu.VMEM((RET_HEADS, RET_QK, RET_V), F32), pltpu.VMEM((ts, nv), BF16)],
        compiler_params=_params("parallel", "arbitrary"),
        name="retention",
    )(x, q, k, v, gate, decay, xi, zeta, gch, gn_g.reshape(1, nv), gn_b.reshape(1, nv), w_out)


def _retention_constants():
    H, C = RET_HEADS, RET_BLOCK
    log_g = jnp.log1p(-jnp.exp2(-5.0 - jnp.arange(H, dtype=F32)))
    idx = jnp.arange(C, dtype=F32)
    diff = idx[:, None] - idx[None, :]
    decay = jnp.where(diff >= 0, jnp.exp(log_g[:, None, None] * jnp.maximum(diff, 0.0)), 0.0)
    xi = jnp.exp(log_g[:, None] * (idx + 1.0))[:, :, None]
    zeta = jnp.exp(log_g[:, None] * (C - 1.0 - idx))[:, None, :]
    gch = jnp.exp(log_g * C)[:, None, None]
    return decay, xi, zeta, gch


def _swap_halves(w):
    half = w.shape[-1] // 2
    return jnp.concatenate([w[..., half:], w[..., :half]], axis=-1)


def _mla_weights(w_in, w_q_b, w_kv_b, q_nope_norm, k_nope_norm, q_rope_norm, k_rope_norm):
    D = w_in.shape[0]
    pad = MLA_HEAD_PAD - QK_NOPE - QK_ROPE
    z = lambda *s: jnp.zeros(s, F32)
    base = 2 * CONV_DIM + Q_LORA + KV_LORA
    w_kr = w_in[:, base:base + QK_ROPE]
    grp = lambda w: jnp.concatenate([z(D, QK_NOPE), w, z(D, pad)], axis=1)
    w_in_p = jnp.concatenate([w_in[:, :base], grp(w_kr), grp(_swap_halves(w_kr))], axis=1)

    wq = w_q_b.reshape(Q_LORA, MLA_HEADS, QK_NOPE + QK_ROPE)
    wq_nope, wq_rope = wq[..., :QK_NOPE], wq[..., QK_NOPE:]
    zq = z(Q_LORA, MLA_HEADS, pad)
    wq_p = jnp.concatenate([wq_nope, wq_rope, zq], axis=-1).reshape(Q_LORA, -1)
    wq_sw = jnp.concatenate([jnp.zeros_like(wq_nope), _swap_halves(wq_rope), zq], axis=-1).reshape(Q_LORA, -1)
    wq_all = jnp.concatenate([wq_p, wq_sw], axis=1)

    wkv = w_kv_b.reshape(KV_LORA, MLA_HEADS, QK_NOPE + V_HEAD)
    wk_p = jnp.concatenate([wkv[..., :QK_NOPE], z(KV_LORA, MLA_HEADS, MLA_HEAD_PAD - QK_NOPE)],
                           axis=-1).reshape(KV_LORA, -1)
    wv_p = wkv[..., QK_NOPE:].reshape(KV_LORA, -1)

    lane = jnp.arange(MLA_HEAD_PAD)
    in_nope = lane < QK_NOPE
    in_rope = (lane >= QK_NOPE) & (lane < QK_NOPE + QK_ROPE)
    seg = (jnp.where(in_nope[:, None] & in_nope[None, :], 1.0 / QK_NOPE, 0.0)
           + jnp.where(in_rope[:, None] & in_rope[None, :], 1.0 / QK_ROPE, 0.0))
    vec = lambda nope, rope: jnp.concatenate([nope, rope, z(pad)]).reshape(1, MLA_HEAD_PAD)
    pair = lambda v: jnp.concatenate([v, v], axis=1)
    gq = pair(vec(q_nope_norm, q_rope_norm))
    gqs = pair(vec(jnp.zeros_like(q_nope_norm), _swap_halves(q_rope_norm)))
    gk = pair(vec(k_nope_norm, z(QK_ROPE)))
    gkr = vec(z(QK_NOPE), k_rope_norm)
    gkrs = vec(z(QK_NOPE), _swap_halves(k_rope_norm))
    zs = jnp.zeros_like(seg)
    seg2 = jnp.concatenate([jnp.concatenate([seg, zs], axis=1), jnp.concatenate([zs, seg], axis=1)], axis=0)
    return (w_in_p.astype(BF16), wq_all.astype(BF16), wk_p.astype(BF16), wv_p.astype(BF16),
            seg2.astype(BF16), gq, gqs, gk, gkr, gkrs)


def kernel(x, mem, positions, ffn1_norm, ffn1_w_gate, ffn1_w_up, ffn1_w_down, ffn2_norm, ffn2_w_gate, ffn2_w_up, ffn2_w_down, mix_norm, xattn_norm, mem_norm, xattn_wq, xattn_wk, xattn_wv, xattn_wo, xattn_q_norm, xattn_k_norm, ev_w_in, ev_conv_w, ev_conv_b, ev_conv_ln_g, ev_conv_ln_b, ev_q_a_norm, ev_w_q_b, ev_kv_a_norm, ev_w_kv_b, ev_q_nope_norm, ev_k_nope_norm, ev_q_rope_norm, ev_k_rope_norm, ev_w_out, od_w_in, od_gn_g, od_gn_b, od_w_out):
    B, S, D = x.shape
    depth = ffn1_norm.shape[0]
    tm = min(TOKEN_TILE, S)
    tf = min(FFN_TOKEN_TILE, B * S)
    te = min(FFN_TOKEN_TILE, S)
    assert S % tm == 0 and (B * S) % tf == 0 and S % min(tf, S) == 0 and S % te == 0 and te % tm == 0
    assert tm % RET_BLOCK == 0 and tm % CONV_HALO == 0 and te % CONV_ROWS == 0
    bf = lambda w: w.astype(BF16)

    h = x.reshape(B * S, D)
    cos_r, sin_r, cm, sm = _rope_tables(positions, tm)
    decay, xi, zeta, gch = _retention_constants()

    for l in range(depth):
        h = _ffn(h, ffn1_norm[l], bf(ffn1_w_gate[l]), bf(ffn1_w_up[l]), bf(ffn1_w_down[l]), tf)
        if l % 2 == 0:
            e = l // 2
            (w_in_p, wq_p, wk_p, wv_p, seg, gq, gqs, gk, gkr, gkrs) = _mla_weights(
                ev_w_in[e], ev_w_q_b[e], ev_w_kv_b[e], ev_q_nope_norm[e], ev_k_nope_norm[e],
                ev_q_rope_norm[e], ev_k_rope_norm[e])
            a, q, k, v = _even_pre(h, mix_norm[l], w_in_p, cm, sm, ev_q_a_norm[e].reshape(1, -1), wq_p,
                                   ev_kv_a_norm[e].reshape(1, -1), wk_p, wv_p, seg, gq, gqs, gk, gkr,
                                   gkrs, te, tm)
            m = _mla_attention(q, k, v, B, S, tm)
            h = _even_post(h, a, m, ev_conv_w[e], ev_conv_b[e], ev_conv_ln_g[e], ev_conv_ln_b[e],
                           bf(ev_w_out[e][:CONV_DIM]), bf(ev_w_out[e][CONV_DIM:]), te, S)
        else:
            o = l // 2
            q, k, v, gate = _odd_pre(h, mix_norm[l], bf(od_w_in[o]), cos_r, sin_r, tm)
            h = _retention(h, q, k, v, gate, decay, xi, zeta, gch, od_gn_g[o], od_gn_b[o],
                           bf(od_w_out[o]), B, S, tm)
        kt, vw = _memkv(mem, mem_norm[l], bf(xattn_wk[l]), bf(xattn_wv[l]), bf(xattn_wo[l]), xattn_k_norm[l])
        h = _xattn(h, xattn_norm[l], bf(xattn_wq[l]), xattn_q_norm[l], kt, vw, min(tf, S), S)
        h = _ffn(h, ffn2_norm[l], bf(ffn2_w_gate[l]), bf(ffn2_w_up[l]), bf(ffn2_w_down[l]), tf)
    return h.reshape(B, S, D)
```

```python
import functools
import math

import jax
import jax.numpy as jnp
from jax import lax
from jax.experimental import pallas as pl
from jax.experimental.pallas import tpu as pltpu

F32 = jnp.float32
BF16 = jnp.bfloat16

EPS = 1e-6
ROPE_THETA = 10000.0

D_FF = 2816
CONV_DIM = 512
CONV_WIDTH = 31
MLA_HEADS = 8
Q_LORA = 256
KV_LORA = 128
QK_NOPE = 64
QK_ROPE = 32
V_HEAD = 64
RET_HEADS = 4
RET_QK = 256
RET_V = 512
X_HEADS = 4

V7X_LANES = 128
V7X_MXU_DIM = 256
V7X_VMEM_LIMIT_BYTES = 56 * 1024 * 1024

TOKEN_TILE = 512
FFN_TOKEN_TILE = 1024
RET_BLOCK = 256
CONV_HALO = 32
MLA_HEAD_PAD = 128


def _params(*sem):
    return pltpu.CompilerParams(dimension_semantics=sem, vmem_limit_bytes=V7X_VMEM_LIMIT_BYTES)


def _const_spec(shape):
    nd = len(shape)
    return pl.BlockSpec(shape, lambda *_: (0,) * nd, pipeline_mode=pl.Buffered(1))


def _rms(x, gain):
    return x * lax.rsqrt(jnp.mean(x * x, axis=-1, keepdims=True) + EPS) * gain


def _silu(x):
    return x / (1.0 + jnp.exp(-x))


def _dot(a, b):
    return jnp.dot(a, b, preferred_element_type=F32)


def _dot_nt(a, b):
    return lax.dot_general(a, b, (((1,), (1,)), ((), ())), preferred_element_type=F32)


def _dot_tn(a, b):
    return lax.dot_general(a, b, (((0,), (0,)), ((), ())), preferred_element_type=F32)


def _select_lanes(t, sel):
    hi = t.astype(BF16)
    rest = t - hi.astype(F32)
    mid = rest.astype(BF16)
    lo = (rest - mid.astype(F32)).astype(BF16)
    return _dot(hi, sel) + _dot(mid, sel) + _dot(lo, sel)


def _rope_kernel(pos_ref, inv_ref, selc_ref, sels_ref, one_ref, cos_ref, sin_ref, cm_ref, sm_ref):
    ang = pos_ref[...].astype(F32) * inv_ref[...]
    cos = jnp.cos(ang)
    sin = jnp.sin(ang)
    cos_ref[...] = cos
    sin_ref[...] = sin
    cm_ref[...] = _select_lanes(cos, selc_ref[...]) + one_ref[...]
    sm_ref[...] = _select_lanes(sin, sels_ref[...])


def _rope_tables(positions, tm):
    T = positions.size
    inv = (ROPE_THETA ** (-jnp.arange(0, RET_QK, 2, dtype=F32) / RET_QK)).reshape(1, RET_QK // 2)
    assert RET_QK % QK_ROPE == 0 and RET_QK // 2 == V7X_LANES
    half = QK_ROPE // 2
    src = jnp.arange(half) * (RET_QK // QK_ROPE)
    lane = jnp.arange(V7X_LANES)
    hit = lambda dst0: (lane[:, None] == src[None, :]).astype(F32) @ \
        (jnp.arange(half)[:, None] + dst0 == lane[None, :]).astype(F32)
    first, second = hit(QK_NOPE), hit(QK_NOPE + half)
    sel_cos = (first + second).astype(BF16)
    sel_sin = (second - first).astype(BF16)
    one = ((lane < QK_NOPE) | (lane >= QK_NOPE + QK_ROPE)).astype(F32).reshape(1, V7X_LANES)
    row = pl.BlockSpec((tm, V7X_LANES), lambda i: (i, 0))
    mat = _const_spec((V7X_LANES, V7X_LANES))
    out = jax.ShapeDtypeStruct((T, V7X_LANES), F32)
    return pl.pallas_call(
        _rope_kernel,
        grid=(T // tm,),
        in_specs=[pl.BlockSpec((tm, 1), lambda i: (i, 0)), _const_spec((1, V7X_LANES)), mat, mat,
                  _const_spec((1, V7X_LANES))],
        out_specs=[row, row, row, row],
        out_shape=[out, out, out, out],
        compiler_params=_params("parallel"),
        name="rope_tables",
    )(positions.reshape(T, 1), inv, sel_cos, sel_sin, one)


def _ff_chunks(d_ff, width):
    edges = list(range(0, d_ff, width)) + [d_ff]
    return [(a, b) for a, b in zip(edges[:-1], edges[1:])]


def _ffn_kernel(x_ref, g_ref, wg_ref, wu_ref, wd_ref, o_ref, h_ref):
    x = x_ref[...]
    xn = _rms(x, g_ref[...]).astype(BF16)
    for a, b in _ff_chunks(h_ref.shape[1], 2 * V7X_MXU_DIM):
        gate = _dot(xn, wg_ref[:, a:b])
        up = _dot(xn, wu_ref[:, a:b])
        h_ref[:, a:b] = (_silu(gate) * up).astype(BF16)
    o_ref[...] = x + 0.5 * _dot(h_ref[...], wd_ref[...])


def _ffn(x, gain, w_gate, w_up, w_down, tm):
    T, D = x.shape
    d_ff = w_gate.shape[1]
    row = pl.BlockSpec((tm, D), lambda i: (i, 0))
    return pl.pallas_call(
        _ffn_kernel,
        grid=(T // tm,),
        in_specs=[row, _const_spec((1, D)), _const_spec((D, d_ff)), _const_spec((D, d_ff)),
                  _const_spec((d_ff, D))],
        out_specs=row,
        out_shape=jax.ShapeDtypeStruct((T, D), F32),
        scratch_shapes=[pltpu.VMEM((tm, d_ff), BF16)],
        compiler_params=_params("parallel"),
        name="ffn",
    )(x, gain.reshape(1, D), w_gate, w_up, w_down)


def _memkv_kernel(mem_ref, g_ref, wk_ref, wv_ref, wo_ref, kn_ref, kt_ref, vw_ref):
    M = mem_ref.shape[0]
    mn = _rms(mem_ref[...], g_ref[...]).astype(BF16)
    k = _dot(mn, wk_ref[...])
    v = _dot(mn, wv_ref[...]).astype(BF16)
    hd = kn_ref.shape[1]
    for h in range(X_HEADS):
        sl = slice(h * hd, (h + 1) * hd)
        kh = _rms(k[:, sl], kn_ref[...])
        kt_ref[0, sl, :] = kh.T.astype(BF16)
        vw_ref[0, h * M:(h + 1) * M, :] = _dot(v[:, sl], wo_ref[sl, :]).astype(BF16)


def _memkv(mem, gain, wk, wv, wo, k_norm):
    B, M, D = mem.shape
    hd = D // X_HEADS
    return pl.pallas_call(
        _memkv_kernel,
        grid=(B,),
        in_specs=[pl.BlockSpec((M, D), lambda b: (b, 0)), _const_spec((1, D)),
                  _const_spec((D, D)), _const_spec((D, D)), _const_spec((D, D)), _const_spec((1, hd))],
        out_specs=[pl.BlockSpec((1, D, M), lambda b: (b, 0, 0)),
                   pl.BlockSpec((1, X_HEADS * M, D), lambda b: (b, 0, 0))],
        out_shape=[jax.ShapeDtypeStruct((B, D, M), BF16), jax.ShapeDtypeStruct((B, X_HEADS * M, D), BF16)],
        compiler_params=_params("parallel"),
        name="xattn_memkv",
    )(mem.reshape(B * M, D), gain.reshape(1, D), wk, wv, wo, k_norm.reshape(1, hd))


def _xattn_kernel(x_ref, g_ref, wq_ref, qn_ref, kt_ref, vw_ref, o_ref, p_ref):
    x = x_ref[...]
    xn = _rms(x, g_ref[...]).astype(BF16)
    q = _dot(xn, wq_ref[...])
    hd = qn_ref.shape[1]
    M = kt_ref.shape[2]
    scale = hd ** -0.5
    for h in range(X_HEADS):
        sl = slice(h * hd, (h + 1) * hd)
        qh = (_rms(q[:, sl], qn_ref[...]) * scale).astype(BF16)
        s = _dot(qh, kt_ref[0, sl, :])
        p = jnp.exp(s - jnp.max(s, axis=-1, keepdims=True))
        p_ref[:, h * M:(h + 1) * M] = (p * (1.0 / jnp.sum(p, axis=-1, keepdims=True))).astype(BF16)
    o_ref[...] = x + _dot(p_ref[...], vw_ref[0])


def _xattn(x, gain, wq, q_norm, kt, vw, tm, seq):
    T, D = x.shape
    M = kt.shape[2]
    hd = D // X_HEADS
    per_seq = seq // tm
    row = pl.BlockSpec((tm, D), lambda i: (i, 0))
    return pl.pallas_call(
        _xattn_kernel,
        grid=(T // tm,),
        in_specs=[row, _const_spec((1, D)), _const_spec((D, D)), _const_spec((1, hd)),
                  pl.BlockSpec((1, D, M), lambda i: (i // per_seq, 0, 0)),
                  pl.BlockSpec((1, X_HEADS * M, D), lambda i: (i // per_seq, 0, 0))],
        out_specs=row,
        out_shape=jax.ShapeDtypeStruct((T, D), F32),
        scratch_shapes=[pltpu.VMEM((tm, X_HEADS * M), BF16)],
        compiler_params=_params("parallel"),
        name="xattn",
    )(x, gain.reshape(1, D), wq, q_norm.reshape(1, hd), kt, vw)


def _even_pre_kernel(x_ref, g_ref, win_ref, cm_ref, sm_ref, qa_ref, wq_ref, kva_ref, wk_ref, wv_ref,
                     seg_ref, gq_ref, gqs_ref, gk_ref, gkr_ref, gkrs_ref,
                     a_ref, qt_ref, k_ref, vt_ref):
    xn = _rms(x_ref[...], g_ref[...]).astype(BF16)
    c = CONV_DIM
    glu = _dot(xn, win_ref[:, 0:2 * c])
    a_ref[...] = glu[:, :c] / (1.0 + jnp.exp(-glu[:, c:]))
    hp = MLA_HEAD_PAD
    z = _dot(xn, win_ref[:, 2 * c:])
    zq = z[:, :Q_LORA]
    zkv = z[:, Q_LORA:Q_LORA + KV_LORA]
    zkr = z[:, Q_LORA + KV_LORA:Q_LORA + KV_LORA + hp]
    zkr_sw = z[:, Q_LORA + KV_LORA + hp:]
    cm = cm_ref[...]
    sm = sm_ref[...]

    rs_kr = lax.rsqrt(jnp.sum(zkr * zkr, axis=-1, keepdims=True) * (1.0 / QK_ROPE) + EPS)
    k_rope = (zkr * rs_kr * gkr_ref[...]) * cm + (zkr_sw * rs_kr * gkrs_ref[...]) * sm

    zq_n = _rms(zq, qa_ref[...]).astype(BF16)
    zkv_n = _rms(zkv, kva_ref[...]).astype(BF16)
    tq = qt_ref.shape[2]
    blocks = [slice(r * tq, (r + 1) * tq) for r in range(qt_ref.shape[0])]
    v = _dot(zkv_n, wv_ref[...])
    for r, rows in enumerate(blocks):
        vt_ref[r] = v[rows].T.astype(BF16)
    pair = lambda t: jnp.concatenate([t, t], axis=1)
    k_rope2 = pair(k_rope)
    q_cos = pair(cm) * (gq_ref[...] * MLA_Q_SCALE)
    q_sin = pair(sm) * (gqs_ref[...] * MLA_Q_SCALE)
    pw = 2 * hp
    nq = MLA_HEADS * hp
    seg = seg_ref[...]
    for h in range(MLA_HEADS // 2):
        sl = slice(h * pw, (h + 1) * pw)
        qh = _dot(zq_n, wq_ref[:, sl])
        qh_sw = _dot(zq_n, wq_ref[:, nq + h * pw:nq + (h + 1) * pw])
        rs = lax.rsqrt(_dot((qh * qh).astype(BF16), seg) + EPS)
        qr = rs * (qh * q_cos + qh_sw * q_sin)
        for r, rows in enumerate(blocks):
            qt_ref[r, sl, :] = qr[rows].T.astype(BF16)
        kh = _dot(zkv_n, wk_ref[:, sl])
        rk = lax.rsqrt(_dot((kh * kh).astype(BF16), seg) + EPS)
        k_ref[:, sl] = (kh * rk * gk_ref[...] + k_rope2).astype(BF16)


def _even_pre(x, gain, w_in_p, cm, sm, qa, wq_p, kva, wk_p, wv_p, seg, gq, gqs, gk, gkr, gkrs, tm, tq):
    T, D = x.shape
    row = lambda w: pl.BlockSpec((tm, w), lambda i: (i, 0))
    hq, hv = MLA_HEADS * MLA_HEAD_PAD, MLA_HEADS * V_HEAD
    consts = [gain.reshape(1, D), w_in_p]
    tables = [cm, sm]
    rest = [qa, wq_p, kva, wk_p, wv_p, seg, gq, gqs, gk, gkr, gkrs]
    return pl.pallas_call(
        _even_pre_kernel,
        grid=(T // tm,),
        in_specs=[row(D)] + [_const_spec(a.shape) for a in consts] + [row(V7X_LANES)] * 2
                 + [_const_spec(a.shape) for a in rest],
        out_specs=[row(CONV_DIM), pl.BlockSpec((tm // tq, hq, tq), lambda i: (i, 0, 0)), row(hq),
                   pl.BlockSpec((tm // tq, hv, tq), lambda i: (i, 0, 0))],
        out_shape=[jax.ShapeDtypeStruct((T, CONV_DIM), F32), jax.ShapeDtypeStruct((T // tq, hq, tq), BF16),
                   jax.ShapeDtypeStruct((T, hq), BF16), jax.ShapeDtypeStruct((T // tq, hv, tq), BF16)],
        compiler_params=_params("parallel"),
        name="even_pre",
    )(x, *consts, *tables, *rest)


MLA_HEADS_PER_STEP = 8
MLA_Q_SCALE = (QK_NOPE + QK_ROPE) ** -0.5 * math.log2(math.e)


MLA_SUM_ROWS = 16
MLA_STAGE_SLOTS = 3


def _mla_kernel(qt_ref, k_ref, vt_ref, o_ref, m_ref, acc_ref, s_ref):
    qi = pl.program_id(2)
    tq = qt_ref.shape[2]
    hp = MLA_HEAD_PAD
    ones = jnp.ones((MLA_SUM_ROWS, tq), BF16)

    m_ref[...] = jnp.full(m_ref.shape, -jnp.inf, F32)
    acc_ref[...] = jnp.zeros(acc_ref.shape, F32)

    def block(j, masked):
        rows = pl.ds(pl.multiple_of(j * tq, tq), tq)
        slots = s_ref.shape[0]

        def stage_scores(h):
            s_ref[h % slots] = _dot(k_ref[rows, h * hp:(h + 1) * hp], qt_ref[0, h * hp:(h + 1) * hp, :])

        def softmax_update(h):
            s = s_ref[h % slots]
            if masked:
                key = lax.broadcasted_iota(jnp.int32, s.shape, 0)
                qry = lax.broadcasted_iota(jnp.int32, s.shape, 1)
                s = jnp.where(key <= qry, s, -jnp.inf)
            m_old = m_ref[h]
            m_new = jnp.maximum(m_old, jnp.max(s, axis=0, keepdims=True))
            p = jnp.exp2(s - m_new).astype(BF16)
            v1 = jnp.concatenate([vt_ref[j, h * V_HEAD:(h + 1) * V_HEAD, :], ones], axis=0)
            acc_ref[h] = jnp.exp2(m_old - m_new) * acc_ref[h] + _dot(v1, p)
            m_ref[h] = m_new

        for h in range(MLA_HEADS_PER_STEP + slots):
            if h >= slots:
                softmax_update(h - slots)
            if h < MLA_HEADS_PER_STEP:
                stage_scores(h)

    def body(j, carry):
        block(j, False)
        return carry

    lax.fori_loop(0, qi, body, 0)
    block(qi, True)
    out = [acc_ref[h, 0:V_HEAD, :] / acc_ref[h, V_HEAD:V_HEAD + 1, :] for h in range(MLA_HEADS_PER_STEP)]
    o_ref[...] = jnp.concatenate(out, axis=0).T.astype(BF16)


def _mla_attention(qt, k, vt, batch, seq, tq):
    T = k.shape[0]
    hs = MLA_HEADS_PER_STEP
    nq = seq // tq
    qw, vw = hs * MLA_HEAD_PAD, hs * V_HEAD
    return pl.pallas_call(
        _mla_kernel,
        grid=(batch, MLA_HEADS // hs, nq),
        in_specs=[pl.BlockSpec((1, qw, tq), lambda b, h, i: (b * nq + i, h, 0)),
                  pl.BlockSpec((seq, qw), lambda b, h, i: (b, h)),
                  pl.BlockSpec((nq, vw, tq), lambda b, h, i: (b, h, 0))],
        out_specs=pl.BlockSpec((tq, vw), lambda b, h, i: (b * nq + i, h)),
        out_shape=jax.ShapeDtypeStruct((T, MLA_HEADS * V_HEAD), BF16),
        scratch_shapes=[pltpu.VMEM((hs, 1, tq), F32), pltpu.VMEM((hs, V_HEAD + MLA_SUM_ROWS, tq), F32),
                        pltpu.VMEM((MLA_STAGE_SLOTS, tq, tq), F32)],
        compiler_params=_params("parallel", "parallel", "arbitrary"),
        name="mla_attention",
    )(qt, k, vt)


CONV_ROWS = 128
V7X_SUBLANES = 8


def _causal_conv(win_ref, cw_ref, ts):
    off = CONV_HALO - (CONV_WIDTH - 1)
    sub = V7X_SUBLANES
    lane_blocks = []
    for lb in range(win_ref.shape[1] // V7X_LANES):
        lanes = slice(lb * V7X_LANES, (lb + 1) * V7X_LANES)
        row_blocks = []
        for base in range(0, ts, CONV_ROWS):
            y = None
            for r in range(sub):
                n = CONV_ROWS + (sub if r else 0)
                z = None
                for q in range(-(-(off + CONV_WIDTH) // sub)):
                    j = sub * q + r - off
                    if 0 <= j < CONV_WIDTH:
                        lo = base + sub * q
                        term = cw_ref[j:j + 1, lanes] * win_ref[lo:lo + n, lanes]
                        z = term if z is None else z + term
                z = z[r:r + CONV_ROWS]
                y = z if y is None else y + z
            row_blocks.append(y)
        lane_blocks.append(jnp.concatenate(row_blocks, axis=0))
    return jnp.concatenate(lane_blocks, axis=1)


def _even_post_kernel(x_ref, halo_ref, a_ref, m_ref, cw_ref, cb_ref, lg_ref, lb_ref, wa_ref, wm_ref,
                      o_ref, win_ref, *, per_seq):
    ts = a_ref.shape[0]
    first = (pl.program_id(0) % per_seq) == 0
    win_ref[0:CONV_HALO, :] = jnp.where(first, 0.0, halo_ref[...])
    win_ref[CONV_HALO:, :] = a_ref[...]
    y = _causal_conv(win_ref, cw_ref, ts) + cb_ref[...]
    mu = jnp.mean(y, axis=-1, keepdims=True)
    yc = y - mu
    var = jnp.mean(yc * yc, axis=-1, keepdims=True)
    act = _silu(yc * lax.rsqrt(var + EPS) * lg_ref[...] + lb_ref[...]).astype(BF16)
    o_ref[...] = x_ref[...] + _dot(act, wa_ref[...]) + _dot(m_ref[...], wm_ref[...])


def _even_post(x, a, m, conv_w, conv_b, ln_g, ln_b, w_out_a, w_out_m, ts, seq):
    T, D = x.shape
    C = a.shape[1]
    per_seq = seq // ts
    halo_blocks = ts // CONV_HALO
    row = lambda w: pl.BlockSpec((ts, w), lambda i: (i, 0))
    return pl.pallas_call(
        functools.partial(_even_post_kernel, per_seq=per_seq),
        grid=(T // ts,),
        in_specs=[row(D),
                  pl.BlockSpec((CONV_HALO, C), lambda i: (jnp.maximum(i * halo_blocks - 1, 0), 0)),
                  row(C), row(C), _const_spec((CONV_WIDTH, C)), _const_spec((1, C)),
                  _const_spec((1, C)), _const_spec((1, C)), _const_spec((C, D)), _const_spec((C, D))],
        out_specs=row(D),
        out_shape=jax.ShapeDtypeStruct((T, D), F32),
        scratch_shapes=[pltpu.VMEM((CONV_HALO + ts, C), F32)],
        compiler_params=_params("parallel"),
        name="even_post",
    )(x, a, a, m, conv_w, conv_b.reshape(1, C), ln_g.reshape(1, C), ln_b.reshape(1, C),
      w_out_a, w_out_m)


def _odd_pre_kernel(x_ref, g_ref, w_ref, cos_ref, sin_ref, q_ref, kt_ref, v_ref, gate_ref):
    xn = _rms(x_ref[...], g_ref[...]).astype(BF16)
    cos = cos_ref[...]
    sin = sin_ref[...]
    half = RET_QK // 2
    nqk = RET_HEADS * RET_QK
    k_scale = RET_QK ** -0.5
    nv = RET_HEADS * RET_V
    for h in range(RET_HEADS):
        sl = slice(h * RET_V, (h + 1) * RET_V)
        gate_ref[:, sl] = _silu(_dot(xn, w_ref[:, 2 * nqk + nv + h * RET_V:2 * nqk + nv + (h + 1) * RET_V]))
    for h in range(RET_HEADS):
        lo = h * RET_QK
        z = _dot(xn, w_ref[:, lo:lo + RET_QK])
        x1, x2 = z[:, :half], z[:, half:]
        q_ref[:, lo:lo + half] = (x1 * cos - x2 * sin).astype(BF16)
        q_ref[:, lo + half:lo + RET_QK] = (x1 * sin + x2 * cos).astype(BF16)
        z = _dot(xn, w_ref[:, nqk + lo:nqk + lo + RET_QK])
        x1, x2 = z[:, :half], z[:, half:]
        k = jnp.concatenate([x1 * cos - x2 * sin, x1 * sin + x2 * cos], axis=1) * k_scale
        kt_ref[0, lo:lo + RET_QK, :] = k.T.astype(BF16)
    for h in range(RET_HEADS):
        sl = slice(h * RET_V, (h + 1) * RET_V)
        v_ref[:, sl] = _dot(xn, w_ref[:, 2 * nqk + h * RET_V:2 * nqk + (h + 1) * RET_V]).astype(BF16)


def _odd_pre(x, gain, w_in, cos_r, sin_r, tm):
    T, D = x.shape
    nqk, nv = RET_HEADS * RET_QK, RET_HEADS * RET_V
    row = lambda w: pl.BlockSpec((tm, w), lambda i: (i, 0))
    return pl.pallas_call(
        _odd_pre_kernel,
        grid=(T // tm,),
        in_specs=[row(D), _const_spec((1, D)), _const_spec(w_in.shape), row(V7X_LANES), row(V7X_LANES)],
        out_specs=[row(nqk), pl.BlockSpec((1, nqk, tm), lambda i: (i, 0, 0)), row(nv), row(nv)],
        out_shape=[jax.ShapeDtypeStruct((T, nqk), BF16), jax.ShapeDtypeStruct((T // tm, nqk, tm), BF16),
                   jax.ShapeDtypeStruct((T, nv), BF16), jax.ShapeDtypeStruct((T, nv), F32)],
        compiler_params=_params("parallel"),
        name="odd_pre",
    )(x, gain.reshape(1, D), w_in, cos_r, sin_r)


def _retention_kernel(x_ref, q_ref, kt_ref, v_ref, gate_ref, decay_ref, xi_ref, zeta_ref, gch_ref,
                      gng_ref, gnb_ref, wo_ref, o_ref, state_ref, y_ref):
    @pl.when(pl.program_id(1) == 0)
    def _():
        state_ref[...] = jnp.zeros(state_ref.shape, F32)

    ts = x_ref.shape[0]
    C = RET_BLOCK
    for c in range(ts // C):
        rows = slice(c * C, (c + 1) * C)
        for h in range(RET_HEADS):
            qk = slice(h * RET_QK, (h + 1) * RET_QK)
            vv = slice(h * RET_V, (h + 1) * RET_V)
            qc = q_ref[rows, qk]
            kt = kt_ref[0, qk, rows]
            vc = v_ref[rows, vv]
            state = state_ref[h]
            scores = (_dot(qc, kt) * decay_ref[h]).astype(BF16)
            out = _dot(scores, vc) + _dot(qc, state.astype(BF16)) * xi_ref[h]
            kz = (kt.astype(F32) * zeta_ref[h]).astype(BF16)
            state_ref[h] = state * gch_ref[h] + _dot(kz, vc)
            mu = jnp.mean(out, axis=-1, keepdims=True)
            oc = out - mu
            var = jnp.mean(oc * oc, axis=-1, keepdims=True)
            gn = oc * lax.rsqrt(var + EPS) * gng_ref[:, vv] + gnb_ref[:, vv]
            y_ref[rows, vv] = (gate_ref[rows, vv] * gn).astype(BF16)
    o_ref[...] = x_ref[...] + _dot(y_ref[...], wo_ref[...])


def _retention(x, q, k, v, gate, decay, xi, zeta, gch, gn_g, gn_b, w_out, batch, seq, ts):
    T, D = x.shape
    nqk, nv = RET_HEADS * RET_QK, RET_HEADS * RET_V
    per_seq = seq // ts
    row = lambda w: pl.BlockSpec((ts, w), lambda b, i: (b * per_seq + i, 0))
    return pl.pallas_call(
        _retention_kernel,
        grid=(batch, per_seq),
        in_specs=[row(D), row(nqk), pl.BlockSpec((1, nqk, ts), lambda b, i: (b * per_seq + i, 0, 0)),
                  row(nv), row(nv), _const_spec(decay.shape), _const_spec(xi.shape), _const_spec(zeta.shape),
                  _const_spec(gch.shape), _const_spec((1, nv)), _const_spec((1, nv)),
                  _const_spec((nv, D))],
        out_specs=row(D),
        out_shape=jax.ShapeDtypeStruct((T, D), F32),
        scratch_shapes=[pltpu.VMEM((RET_HEADS, RET_QK, RET_V), F32), pltpu.VMEM((ts, nv), BF16)],
        compiler_params=_params("parallel", "arbitrary"),
        name="retention",
    )(x, q, k, v, gate, decay, xi, zeta, gch, gn_g.reshape(1, nv), gn_b.reshape(1, nv), w_out)


def _retention_constants():
    H, C = RET_HEADS, RET_BLOCK
    log_g = jnp.log1p(-jnp.exp2(-5.0 - jnp.arange(H, dtype=F32)))
    idx = jnp.arange(C, dtype=F32)
    diff = idx[:, None] - idx[None, :]
    decay = jnp.where(diff >= 0, jnp.exp(log_g[:, None, None] * jnp.maximum(diff, 0.0)), 0.0)
    xi = jnp.exp(log_g[:, None] * (idx + 1.0))[:, :, None]
    zeta = jnp.exp(log_g[:, None] * (C - 1.0 - idx))[:, None, :]
    gch = jnp.exp(log_g * C)[:, None, None]
    return decay, xi, zeta, gch


def _swap_halves(w):
    half = w.shape[-1] // 2
    return jnp.concatenate([w[..., half:], w[..., :half]], axis=-1)


def _mla_weights(w_in, w_q_b, w_kv_b, q_nope_norm, k_nope_norm, q_rope_norm, k_rope_norm):
    D = w_in.shape[0]
    pad = MLA_HEAD_PAD - QK_NOPE - QK_ROPE
    z = lambda *s: jnp.zeros(s, F32)
    base = 2 * CONV_DIM + Q_LORA + KV_LORA
    w_kr = w_in[:, base:base + QK_ROPE]
    grp = lambda w: jnp.concatenate([z(D, QK_NOPE), w, z(D, pad)], axis=1)
    w_in_p = jnp.concatenate([w_in[:, :base], grp(w_kr), grp(_swap_halves(w_kr))], axis=1)

    wq = w_q_b.reshape(Q_LORA, MLA_HEADS, QK_NOPE + QK_ROPE)
    wq_nope, wq_rope = wq[..., :QK_NOPE], wq[..., QK_NOPE:]
    zq = z(Q_LORA, MLA_HEADS, pad)
    wq_p = jnp.concatenate([wq_nope, wq_rope, zq], axis=-1).reshape(Q_LORA, -1)
    wq_sw = jnp.concatenate([jnp.zeros_like(wq_nope), _swap_halves(wq_rope), zq], axis=-1).reshape(Q_LORA, -1)
    wq_all = jnp.concatenate([wq_p, wq_sw], axis=1)

    wkv = w_kv_b.reshape(KV_LORA, MLA_HEADS, QK_NOPE + V_HEAD)
    wk_p = jnp.concatenate([wkv[..., :QK_NOPE], z(KV_LORA, MLA_HEADS, MLA_HEAD_PAD - QK_NOPE)],
                           axis=-1).reshape(KV_LORA, -1)
    wv_p = wkv[..., QK_NOPE:].reshape(KV_LORA, -1)

    lane = jnp.arange(MLA_HEAD_PAD)
    in_nope = lane < QK_NOPE
    in_rope = (lane >= QK_NOPE) & (lane < QK_NOPE + QK_ROPE)
    seg = (jnp.where(in_nope[:, None] & in_nope[None, :], 1.0 / QK_NOPE, 0.0)
           + jnp.where(in_rope[:, None] & in_rope[None, :], 1.0 / QK_ROPE, 0.0))
    vec = lambda nope, rope: jnp.concatenate([nope, rope, z(pad)]).reshape(1, MLA_HEAD_PAD)
    pair = lambda v: jnp.concatenate([v, v], axis=1)
    gq = pair(vec(q_nope_norm, q_rope_norm))
    gqs = pair(vec(jnp.zeros_like(q_nope_norm), _swap_halves(q_rope_norm)))
    gk = pair(vec(k_nope_norm, z(QK_ROPE)))
    gkr = vec(z(QK_NOPE), k_rope_norm)
    gkrs = vec(z(QK_NOPE), _swap_halves(k_rope_norm))
    zs = jnp.zeros_like(seg)
    seg2 = jnp.concatenate([jnp.concatenate([seg, zs], axis=1), jnp.concatenate([zs, seg], axis=1)], axis=0)
    return (w_in_p.astype(BF16), wq_all.astype(BF16), wk_p.astype(BF16), wv_p.astype(BF16),
            seg2.astype(BF16), gq, gqs, gk, gkr, gkrs)


def kernel(x, mem, positions, ffn1_norm, ffn1_w_gate, ffn1_w_up, ffn1_w_down, ffn2_norm, ffn2_w_gate, ffn2_w_up, ffn2_w_down, mix_norm, xattn_norm, mem_norm, xattn_wq, xattn_wk, xattn_wv, xattn_wo, xattn_q_norm, xattn_k_norm, ev_w_in, ev_conv_w, ev_conv_b, ev_conv_ln_g, ev_conv_ln_b, ev_q_a_norm, ev_w_q_b, ev_kv_a_norm, ev_w_kv_b, ev_q_nope_norm, ev_k_nope_norm, ev_q_rope_norm, ev_k_rope_norm, ev_w_out, od_w_in, od_gn_g, od_gn_b, od_w_out):
    B, S, D = x.shape
    depth = ffn1_norm.shape[0]
    tm = min(TOKEN_TILE, S)
    tf = min(FFN_TOKEN_TILE, B * S)
    te = min(FFN_TOKEN_TILE, S)
    assert S % tm == 0 and (B * S) % tf == 0 and S % min(tf, S) == 0 and S % te == 0 and te % tm == 0
    assert tm % RET_BLOCK == 0 and tm % CONV_HALO == 0 and te % CONV_ROWS == 0
    bf = lambda w: w.astype(BF16)

    h = x.reshape(B * S, D)
    cos_r, sin_r, cm, sm = _rope_tables(positions, tm)
    decay, xi, zeta, gch = _retention_constants()

    for l in range(depth):
        h = _ffn(h, ffn1_norm[l], bf(ffn1_w_gate[l]), bf(ffn1_w_up[l]), bf(ffn1_w_down[l]), tf)
        if l % 2 == 0:
            e = l // 2
            (w_in_p, wq_p, wk_p, wv_p, seg, gq, gqs, gk, gkr, gkrs) = _mla_weights(
                ev_w_in[e], ev_w_q_b[e], ev_w_kv_b[e], ev_q_nope_norm[e], ev_k_nope_norm[e],
                ev_q_rope_norm[e], ev_k_rope_norm[e])
            a, q, k, v = _even_pre(h, mix_norm[l], w_in_p, cm, sm, ev_q_a_norm[e].reshape(1, -1), wq_p,
                                   ev_kv_a_norm[e].reshape(1, -1), wk_p, wv_p, seg, gq, gqs, gk, gkr,
                                   gkrs, te, tm)
            m = _mla_attention(q, k, v, B, S, tm)
            h = _even_post(h, a, m, ev_conv_w[e], ev_conv_b[e], ev_conv_ln_g[e], ev_conv_ln_b[e],
                           bf(ev_w_out[e][:CONV_DIM]), bf(ev_w_out[e][CONV_DIM:]), te, S)
        else:
            o = l // 2
            q, k, v, gate = _odd_pre(h, mix_norm[l], bf(od_w_in[o]), cos_r, sin_r, tm)
            h = _retention(h, q, k, v, gate, decay, xi, zeta, gch, od_gn_g[o], od_gn_b[o],
                           bf(od_w_out[o]), B, S, tm)
        kt, vw = _memkv(mem, mem_norm[l], bf(xattn_wk[l]), bf(xattn_wv[l]), bf(xattn_wo[l]), xattn_k_norm[l])
        h = _xattn(h, xattn_norm[l], bf(xattn_wq[l]), xattn_q_norm[l], kt, vw, min(tf, S), S)
        h = _ffn(h, ffn2_norm[l], bf(ffn2_w_gate[l]), bf(ffn2_w_up[l]), bf(ffn2_w_down[l]), tf)
    return h.reshape(B, S, D)
```

```python
import functools
import math

import jax
import jax.numpy as jnp
from jax import lax
from jax.experimental import pallas as pl
from jax.experimental.pallas import tpu as pltpu

F32 = jnp.float32
BF16 = jnp.bfloat16

EPS = 1e-6
ROPE_THETA = 10000.0

D_FF = 2816
CONV_DIM = 512
CONV_WIDTH = 31
MLA_HEADS = 8
Q_LORA = 256
KV_LORA = 128
QK_NOPE = 64
QK_ROPE = 32
V_HEAD = 64
RET_HEADS = 4
RET_QK = 256
RET_V = 512
X_HEADS = 4

V7X_LANES = 128
V7X_MXU_DIM = 256
V7X_VMEM_LIMIT_BYTES = 56 * 1024 * 1024

TOKEN_TILE = 512
ROPE_TOKEN_TILE = 2048
FFN_TOKEN_TILE = 1024
RET_BLOCK = 256
CONV_HALO = 32
MLA_HEAD_PAD = 128


def _params(*sem):
    return pltpu.CompilerParams(dimension_semantics=sem, vmem_limit_bytes=V7X_VMEM_LIMIT_BYTES)


def _const_spec(shape):
    nd = len(shape)
    return pl.BlockSpec(shape, lambda *_: (0,) * nd, pipeline_mode=pl.Buffered(1))


def _rms(x, gain):
    return x * lax.rsqrt(jnp.mean(x * x, axis=-1, keepdims=True) + EPS) * gain


def _silu(x):
    return x / (1.0 + jnp.exp(-x))


def _dot(a, b):
    return jnp.dot(a, b, preferred_element_type=F32)


def _dot_nt(a, b):
    return lax.dot_general(a, b, (((1,), (1,)), ((), ())), preferred_element_type=F32)


def _dot_tn(a, b):
    return lax.dot_general(a, b, (((0,), (0,)), ((), ())), preferred_element_type=F32)


def _select_lanes(t, sel):
    hi = t.astype(BF16)
    rest = t - hi.astype(F32)
    mid = rest.astype(BF16)
    lo = (rest - mid.astype(F32)).astype(BF16)
    return _dot(hi, sel) + _dot(mid, sel) + _dot(lo, sel)


def _rope_kernel(pos_ref, inv_ref, selc_ref, sels_ref, one_ref, cos_ref, sin_ref, cm_ref, sm_ref):
    ang = pos_ref[...].astype(F32) * inv_ref[...]
    cos = jnp.cos(ang)
    sin = jnp.sin(ang)
    cos_ref[...] = cos
    sin_ref[...] = sin
    cm_ref[...] = _select_lanes(cos, selc_ref[...]) + one_ref[...]
    sm_ref[...] = _select_lanes(sin, sels_ref[...])


def _rope_tables(positions, tm):
    T = positions.size
    inv = (ROPE_THETA ** (-jnp.arange(0, RET_QK, 2, dtype=F32) / RET_QK)).reshape(1, RET_QK // 2)
    assert RET_QK % QK_ROPE == 0 and RET_QK // 2 == V7X_LANES
    half = QK_ROPE // 2
    src = jnp.arange(half) * (RET_QK // QK_ROPE)
    lane = jnp.arange(V7X_LANES)
    hit = lambda dst0: (lane[:, None] == src[None, :]).astype(F32) @ \
        (jnp.arange(half)[:, None] + dst0 == lane[None, :]).astype(F32)
    first, second = hit(QK_NOPE), hit(QK_NOPE + half)
    sel_cos = (first + second).astype(BF16)
    sel_sin = (second - first).astype(BF16)
    one = ((lane < QK_NOPE) | (lane >= QK_NOPE + QK_ROPE)).astype(F32).reshape(1, V7X_LANES)
    row = pl.BlockSpec((tm, V7X_LANES), lambda i: (i, 0))
    mat = _const_spec((V7X_LANES, V7X_LANES))
    out = jax.ShapeDtypeStruct((T, V7X_LANES), F32)
    return pl.pallas_call(
        _rope_kernel,
        grid=(T // tm,),
        in_specs=[pl.BlockSpec((tm, 1), lambda i: (i, 0)), _const_spec((1, V7X_LANES)), mat, mat,
                  _const_spec((1, V7X_LANES))],
        out_specs=[row, row, row, row],
        out_shape=[out, out, out, out],
        compiler_params=_params("parallel"),
        name="rope_tables",
    )(positions.reshape(T, 1), inv, sel_cos, sel_sin, one)


def _ff_chunks(d_ff, width):
    edges = list(range(0, d_ff, width)) + [d_ff]
    return [(a, b) for a, b in zip(edges[:-1], edges[1:])]


def _ffn_kernel(x_ref, g_ref, wg_ref, wu_ref, wd_ref, o_ref, h_ref):
    x = x_ref[...]
    xn = _rms(x, g_ref[...]).astype(BF16)
    for a, b in _ff_chunks(h_ref.shape[1], 2 * V7X_MXU_DIM):
        gate = _dot(xn, wg_ref[:, a:b])
        up = _dot(xn, wu_ref[:, a:b])
        h_ref[:, a:b] = (_silu(gate) * up).astype(BF16)
    o_ref[...] = x + 0.5 * _dot(h_ref[...], wd_ref[...])


def _ffn(x, gain, w_gate, w_up, w_down, tm):
    T, D = x.shape
    d_ff = w_gate.shape[1]
    row = pl.BlockSpec((tm, D), lambda i: (i, 0))
    return pl.pallas_call(
        _ffn_kernel,
        grid=(T // tm,),
        in_specs=[row, _const_spec((1, D)), _const_spec((D, d_ff)), _const_spec((D, d_ff)),
                  _const_spec((d_ff, D))],
        out_specs=row,
        out_shape=jax.ShapeDtypeStruct((T, D), F32),
        scratch_shapes=[pltpu.VMEM((tm, d_ff), BF16)],
        compiler_params=_params("parallel"),
        name="ffn",
    )(x, gain.reshape(1, D), w_gate, w_up, w_down)


def _memkv_kernel(mem_ref, g_ref, wk_ref, wv_ref, wo_ref, kn_ref, kt_ref, vw_ref):
    M = mem_ref.shape[0]
    mn = _rms(mem_ref[...], g_ref[...]).astype(BF16)
    k = _dot(mn, wk_ref[...])
    v = _dot(mn, wv_ref[...]).astype(BF16)
    hd = kn_ref.shape[1]
    for h in range(X_HEADS):
        sl = slice(h * hd, (h + 1) * hd)
        kh = _rms(k[:, sl], kn_ref[...])
        kt_ref[0, sl, :] = kh.T.astype(BF16)
        vw_ref[0, h * M:(h + 1) * M, :] = _dot(v[:, sl], wo_ref[sl, :]).astype(BF16)


def _memkv(mem, gain, wk, wv, wo, k_norm):
    B, M, D = mem.shape
    hd = D // X_HEADS
    return pl.pallas_call(
        _memkv_kernel,
        grid=(B,),
        in_specs=[pl.BlockSpec((M, D), lambda b: (b, 0)), _const_spec((1, D)),
                  _const_spec((D, D)), _const_spec((D, D)), _const_spec((D, D)), _const_spec((1, hd))],
        out_specs=[pl.BlockSpec((1, D, M), lambda b: (b, 0, 0)),
                   pl.BlockSpec((1, X_HEADS * M, D), lambda b: (b, 0, 0))],
        out_shape=[jax.ShapeDtypeStruct((B, D, M), BF16), jax.ShapeDtypeStruct((B, X_HEADS * M, D), BF16)],
        compiler_params=_params("parallel"),
        name="xattn_memkv",
    )(mem.reshape(B * M, D), gain.reshape(1, D), wk, wv, wo, k_norm.reshape(1, hd))


def _xattn_kernel(x_ref, g_ref, wq_ref, qn_ref, kt_ref, vw_ref, o_ref, p_ref):
    x = x_ref[...]
    xn = _rms(x, g_ref[...]).astype(BF16)
    q = _dot(xn, wq_ref[...])
    hd = qn_ref.shape[1]
    M = kt_ref.shape[2]
    scale = hd ** -0.5
    for h in range(X_HEADS):
        sl = slice(h * hd, (h + 1) * hd)
        qh = (_rms(q[:, sl], qn_ref[...]) * scale).astype(BF16)
        s = _dot(qh, kt_ref[0, sl, :])
        p = jnp.exp(s - jnp.max(s, axis=-1, keepdims=True))
        p_ref[:, h * M:(h + 1) * M] = (p * (1.0 / jnp.sum(p, axis=-1, keepdims=True))).astype(BF16)
    o_ref[...] = x + _dot(p_ref[...], vw_ref[0])


def _xattn(x, gain, wq, q_norm, kt, vw, tm, seq):
    T, D = x.shape
    M = kt.shape[2]
    hd = D // X_HEADS
    per_seq = seq // tm
    row = pl.BlockSpec((tm, D), lambda i: (i, 0))
    return pl.pallas_call(
        _xattn_kernel,
        grid=(T // tm,),
        in_specs=[row, _const_spec((1, D)), _const_spec((D, D)), _const_spec((1, hd)),
                  pl.BlockSpec((1, D, M), lambda i: (i // per_seq, 0, 0)),
                  pl.BlockSpec((1, X_HEADS * M, D), lambda i: (i // per_seq, 0, 0))],
        out_specs=row,
        out_shape=jax.ShapeDtypeStruct((T, D), F32),
        scratch_shapes=[pltpu.VMEM((tm, X_HEADS * M), BF16)],
        compiler_params=_params("parallel"),
        name="xattn",
    )(x, gain.reshape(1, D), wq, q_norm.reshape(1, hd), kt, vw)


def _even_pre_kernel(x_ref, g_ref, win_ref, cm_ref, sm_ref, qa_ref, wq_ref, kva_ref, wk_ref, wv_ref,
                     seg_ref, gq_ref, gqs_ref, gk_ref, gkr_ref, gkrs_ref,
                     a_ref, qt_ref, k_ref, vt_ref):
    xn = _rms(x_ref[...], g_ref[...]).astype(BF16)
    c = CONV_DIM
    glu = _dot(xn, win_ref[:, 0:2 * c])
    a_ref[...] = glu[:, :c] / (1.0 + jnp.exp(-glu[:, c:]))
    hp = MLA_HEAD_PAD
    z = _dot(xn, win_ref[:, 2 * c:])
    zq = z[:, :Q_LORA]
    zkv = z[:, Q_LORA:Q_LORA + KV_LORA]
    zkr = z[:, Q_LORA + KV_LORA:Q_LORA + KV_LORA + hp]
    zkr_sw = z[:, Q_LORA + KV_LORA + hp:]
    cm = cm_ref[...]
    sm = sm_ref[...]

    rs_kr = lax.rsqrt(jnp.sum(zkr * zkr, axis=-1, keepdims=True) * (1.0 / QK_ROPE) + EPS)
    k_rope = (zkr * rs_kr * gkr_ref[...]) * cm + (zkr_sw * rs_kr * gkrs_ref[...]) * sm

    zq_n = _rms(zq, qa_ref[...]).astype(BF16)
    zkv_n = _rms(zkv, kva_ref[...]).astype(BF16)
    tq = qt_ref.shape[2]
    blocks = [slice(r * tq, (r + 1) * tq) for r in range(qt_ref.shape[0])]
    v = _dot(zkv_n, wv_ref[...])
    for r, rows in enumerate(blocks):
        vt_ref[r] = v[rows].T.astype(BF16)
    pair = lambda t: jnp.concatenate([t, t], axis=1)
    k_rope2 = pair(k_rope)
    q_cos = pair(cm) * (gq_ref[...] * MLA_Q_SCALE)
    q_sin = pair(sm) * (gqs_ref[...] * MLA_Q_SCALE)
    pw = 2 * hp
    nq = MLA_HEADS * hp
    seg = seg_ref[...]
    for h in range(MLA_HEADS // 2):
        sl = slice(h * pw, (h + 1) * pw)
        qh = _dot(zq_n, wq_ref[:, sl])
        qh_sw = _dot(zq_n, wq_ref[:, nq + h * pw:nq + (h + 1) * pw])
        rs = lax.rsqrt(_dot((qh * qh).astype(BF16), seg) + EPS)
        qr = rs * (qh * q_cos + qh_sw * q_sin)
        for r, rows in enumerate(blocks):
            qt_ref[r, sl, :] = qr[rows].T.astype(BF16)
        kh = _dot(zkv_n, wk_ref[:, sl])
        rk = lax.rsqrt(_dot((kh * kh).astype(BF16), seg) + EPS)
        k_ref[:, sl] = (kh * rk * gk_ref[...] + k_rope2).astype(BF16)


def _even_pre(x, gain, w_in_p, cm, sm, qa, wq_p, kva, wk_p, wv_p, seg, gq, gqs, gk, gkr, gkrs, tm, tq):
    T, D = x.shape
    row = lambda w: pl.BlockSpec((tm, w), lambda i: (i, 0))
    hq, hv = MLA_HEADS * MLA_HEAD_PAD, MLA_HEADS * V_HEAD
    consts = [gain.reshape(1, D), w_in_p]
    tables = [cm, sm]
    rest = [qa, wq_p, kva, wk_p, wv_p, seg, gq, gqs, gk, gkr, gkrs]
    return pl.pallas_call(
        _even_pre_kernel,
        grid=(T // tm,),
        in_specs=[row(D)] + [_const_spec(a.shape) for a in consts] + [row(V7X_LANES)] * 2
                 + [_const_spec(a.shape) for a in rest],
        out_specs=[row(CONV_DIM), pl.BlockSpec((tm // tq, hq, tq), lambda i: (i, 0, 0)), row(hq),
                   pl.BlockSpec((tm // tq, hv, tq), lambda i: (i, 0, 0))],
        out_shape=[jax.ShapeDtypeStruct((T, CONV_DIM), F32), jax.ShapeDtypeStruct((T // tq, hq, tq), BF16),
                   jax.ShapeDtypeStruct((T, hq), BF16), jax.ShapeDtypeStruct((T // tq, hv, tq), BF16)],
        compiler_params=_params("parallel"),
        name="even_pre",
    )(x, *consts, *tables, *rest)


MLA_HEADS_PER_STEP = 8
MLA_Q_SCALE = (QK_NOPE + QK_ROPE) ** -0.5 * math.log2(math.e)


MLA_SUM_ROWS = 16
MLA_STAGE_SLOTS = 3


def _mla_kernel(qt_ref, k_ref, vt_ref, o_ref, m_ref, acc_ref, s_ref):
    qi = pl.program_id(2)
    tq = qt_ref.shape[2]
    hp = MLA_HEAD_PAD
    ones = jnp.ones((MLA_SUM_ROWS, tq), BF16)

    m_ref[...] = jnp.full(m_ref.shape, -jnp.inf, F32)
    acc_ref[...] = jnp.zeros(acc_ref.shape, F32)

    def block(j, masked):
        rows = pl.ds(pl.multiple_of(j * tq, tq), tq)
        slots = s_ref.shape[0]

        def stage_scores(h):
            s_ref[h % slots] = _dot(k_ref[rows, h * hp:(h + 1) * hp], qt_ref[0, h * hp:(h + 1) * hp, :])

        def softmax_update(h):
            s = s_ref[h % slots]
            if masked:
                key = lax.broadcasted_iota(jnp.int32, s.shape, 0)
                qry = lax.broadcasted_iota(jnp.int32, s.shape, 1)
                s = jnp.where(key <= qry, s, -jnp.inf)
            m_old = m_ref[h]
            m_new = jnp.maximum(m_old, jnp.max(s, axis=0, keepdims=True))
            p = jnp.exp2(s - m_new).astype(BF16)
            v1 = jnp.concatenate([vt_ref[j, h * V_HEAD:(h + 1) * V_HEAD, :], ones], axis=0)
            acc_ref[h] = jnp.exp2(m_old - m_new) * acc_ref[h] + _dot(v1, p)
            m_ref[h] = m_new

        for h in range(MLA_HEADS_PER_STEP + slots):
            if h >= slots:
                softmax_update(h - slots)
            if h < MLA_HEADS_PER_STEP:
                stage_scores(h)

    def body(j, carry):
        block(j, False)
        return carry

    lax.fori_loop(0, qi, body, 0)
    block(qi, True)
    out = [acc_ref[h, 0:V_HEAD, :] / acc_ref[h, V_HEAD:V_HEAD + 1, :] for h in range(MLA_HEADS_PER_STEP)]
    o_ref[...] = jnp.concatenate(out, axis=0).T.astype(BF16)


def _mla_attention(qt, k, vt, batch, seq, tq):
    T = k.shape[0]
    hs = MLA_HEADS_PER_STEP
    nq = seq // tq
    qw, vw = hs * MLA_HEAD_PAD, hs * V_HEAD
    return pl.pallas_call(
        _mla_kernel,
        grid=(batch, MLA_HEADS // hs, nq),
        in_specs=[pl.BlockSpec((1, qw, tq), lambda b, h, i: (b * nq + i, h, 0)),
                  pl.BlockSpec((seq, qw), lambda b, h, i: (b, h)),
                  pl.BlockSpec((nq, vw, tq), lambda b, h, i: (b, h, 0))],
        out_specs=pl.BlockSpec((tq, vw), lambda b, h, i: (b * nq + i, h)),
        out_shape=jax.ShapeDtypeStruct((T, MLA_HEADS * V_HEAD), BF16),
        scratch_shapes=[pltpu.VMEM((hs, 1, tq), F32), pltpu.VMEM((hs, V_HEAD + MLA_SUM_ROWS, tq), F32),
                        pltpu.VMEM((MLA_STAGE_SLOTS, tq, tq), F32)],
        compiler_params=_params("parallel", "parallel", "arbitrary"),
        name="mla_attention",
    )(qt, k, vt)


CONV_ROWS = 128
V7X_SUBLANES = 8


def _causal_conv(win_ref, cw_ref, ts):
    off = CONV_HALO - (CONV_WIDTH - 1)
    sub = V7X_SUBLANES
    lane_blocks = []
    for lb in range(win_ref.shape[1] // V7X_LANES):
        lanes = slice(lb * V7X_LANES, (lb + 1) * V7X_LANES)
        row_blocks = []
        for base in range(0, ts, CONV_ROWS):
            y = None
            for r in range(sub):
                n = CONV_ROWS + (sub if r else 0)
                z = None
                for q in range(-(-(off + CONV_WIDTH) // sub)):
                    j = sub * q + r - off
                    if 0 <= j < CONV_WIDTH:
                        lo = base + sub * q
                        term = cw_ref[j:j + 1, lanes] * win_ref[lo:lo + n, lanes]
                        z = term if z is None else z + term
                z = z[r:r + CONV_ROWS]
                y = z if y is None else y + z
            row_blocks.append(y)
        lane_blocks.append(jnp.concatenate(row_blocks, axis=0))
    return jnp.concatenate(lane_blocks, axis=1)


def _even_post_kernel(x_ref, halo_ref, a_ref, m_ref, cw_ref, cb_ref, lg_ref, lb_ref, wa_ref, wm_ref,
                      o_ref, win_ref, *, per_seq):
    ts = a_ref.shape[0]
    first = (pl.program_id(0) % per_seq) == 0
    win_ref[0:CONV_HALO, :] = jnp.where(first, 0.0, halo_ref[...])
    win_ref[CONV_HALO:, :] = a_ref[...]
    y = _causal_conv(win_ref, cw_ref, ts) + cb_ref[...]
    mu = jnp.mean(y, axis=-1, keepdims=True)
    yc = y - mu
    var = jnp.mean(yc * yc, axis=-1, keepdims=True)
    act = _silu(yc * lax.rsqrt(var + EPS) * lg_ref[...] + lb_ref[...]).astype(BF16)
    o_ref[...] = x_ref[...] + _dot(act, wa_ref[...]) + _dot(m_ref[...], wm_ref[...])


def _even_post(x, a, m, conv_w, conv_b, ln_g, ln_b, w_out_a, w_out_m, ts, seq):
    T, D = x.shape
    C = a.shape[1]
    per_seq = seq // ts
    halo_blocks = ts // CONV_HALO
    row = lambda w: pl.BlockSpec((ts, w), lambda i: (i, 0))
    return pl.pallas_call(
        functools.partial(_even_post_kernel, per_seq=per_seq),
        grid=(T // ts,),
        in_specs=[row(D),
                  pl.BlockSpec((CONV_HALO, C), lambda i: (jnp.maximum(i * halo_blocks - 1, 0), 0)),
                  row(C), row(C), _const_spec((CONV_WIDTH, C)), _const_spec((1, C)),
                  _const_spec((1, C)), _const_spec((1, C)), _const_spec((C, D)), _const_spec((C, D))],
        out_specs=row(D),
        out_shape=jax.ShapeDtypeStruct((T, D), F32),
        scratch_shapes=[pltpu.VMEM((CONV_HALO + ts, C), F32)],
        compiler_params=_params("parallel"),
        name="even_post",
    )(x, a, a, m, conv_w, conv_b.reshape(1, C), ln_g.reshape(1, C), ln_b.reshape(1, C),
      w_out_a, w_out_m)


def _odd_pre_kernel(x_ref, g_ref, w_ref, cos_ref, sin_ref, q_ref, kt_ref, v_ref, gate_ref):
    xn = _rms(x_ref[...], g_ref[...]).astype(BF16)
    cos = cos_ref[...]
    sin = sin_ref[...]
    half = RET_QK // 2
    nqk = RET_HEADS * RET_QK
    k_scale = RET_QK ** -0.5
    nv = RET_HEADS * RET_V
    for h in range(RET_HEADS):
        sl = slice(h * RET_V, (h + 1) * RET_V)
        gate_ref[:, sl] = _silu(_dot(xn, w_ref[:, 2 * nqk + nv + h * RET_V:2 * nqk + nv + (h + 1) * RET_V]))
    for h in range(RET_HEADS):
        lo = h * RET_QK
        z = _dot(xn, w_ref[:, lo:lo + RET_QK])
        x1, x2 = z[:, :half], z[:, half:]
        q_ref[:, lo:lo + half] = (x1 * cos - x2 * sin).astype(BF16)
        q_ref[:, lo + half:lo + RET_QK] = (x1 * sin + x2 * cos).astype(BF16)
        z = _dot(xn, w_ref[:, nqk + lo:nqk + lo + RET_QK])
        x1, x2 = z[:, :half], z[:, half:]
        k = jnp.concatenate([x1 * cos - x2 * sin, x1 * sin + x2 * cos], axis=1) * k_scale
        kt_ref[0, lo:lo + RET_QK, :] = k.T.astype(BF16)
    for h in range(RET_HEADS):
        sl = slice(h * RET_V, (h + 1) * RET_V)
        v_ref[:, sl] = _dot(xn, w_ref[:, 2 * nqk + h * RET_V:2 * nqk + (h + 1) * RET_V]).astype(BF16)


def _odd_pre(x, gain, w_in, cos_r, sin_r, tm):
    T, D = x.shape
    nqk, nv = RET_HEADS * RET_QK, RET_HEADS * RET_V
    row = lambda w: pl.BlockSpec((tm, w), lambda i: (i, 0))
    return pl.pallas_call(
        _odd_pre_kernel,
        grid=(T // tm,),
        in_specs=[row(D), _const_spec((1, D)), _const_spec(w_in.shape), row(V7X_LANES), row(V7X_LANES)],
        out_specs=[row(nqk), pl.BlockSpec((1, nqk, tm), lambda i: (i, 0, 0)), row(nv), row(nv)],
        out_shape=[jax.ShapeDtypeStruct((T, nqk), BF16), jax.ShapeDtypeStruct((T // tm, nqk, tm), BF16),
                   jax.ShapeDtypeStruct((T, nv), BF16), jax.ShapeDtypeStruct((T, nv), F32)],
        compiler_params=_params("parallel"),
        name="odd_pre",
    )(x, gain.reshape(1, D), w_in, cos_r, sin_r)


def _retention_kernel(x_ref, q_ref, kt_ref, v_ref, gate_ref, decay_ref, xi_ref, zeta_ref, gch_ref,
                      gng_ref, gnb_ref, wo_ref, o_ref, state_ref, y_ref):
    @pl.when(pl.program_id(1) == 0)
    def _():
        state_ref[...] = jnp.zeros(state_ref.shape, F32)

    ts = x_ref.shape[0]
    C = RET_BLOCK
    for c in range(ts // C):
        rows = slice(c * C, (c + 1) * C)
        for h in range(RET_HEADS):
            qk = slice(h * RET_QK, (h + 1) * RET_QK)
            vv = slice(h * RET_V, (h + 1) * RET_V)
            qc = q_ref[rows, qk]
            kt = kt_ref[0, qk, rows]
            vc = v_ref[rows, vv]
            state = state_ref[h]
            scores = (_dot(qc, kt) * decay_ref[h]).astype(BF16)
            out = _dot(scores, vc) + _dot(qc, state.astype(BF16)) * xi_ref[h]
            kz = (kt.astype(F32) * zeta_ref[h]).astype(BF16)
            state_ref[h] = state * gch_ref[h] + _dot(kz, vc)
            mu = jnp.mean(out, axis=-1, keepdims=True)
            oc = out - mu
            var = jnp.mean(oc * oc, axis=-1, keepdims=True)
            gn = oc * lax.rsqrt(var + EPS) * gng_ref[:, vv] + gnb_ref[:, vv]
            y_ref[rows, vv] = (gate_ref[rows, vv] * gn).astype(BF16)
    o_ref[...] = x_ref[...] + _dot(y_ref[...], wo_ref[...])


def _retention(x, q, k, v, gate, decay, xi, zeta, gch, gn_g, gn_b, w_out, batch, seq, ts):
    T, D = x.shape
    nqk, nv = RET_HEADS * RET_QK, RET_HEADS * RET_V
    per_seq = seq // ts
    row = lambda w: pl.BlockSpec((ts, w), lambda b, i: (b * per_seq + i, 0))
    return pl.pallas_call(
        _retention_kernel,
        grid=(batch, per_seq),
        in_specs=[row(D), row(nqk), pl.BlockSpec((1, nqk, ts), lambda b, i: (b * per_seq + i, 0, 0)),
                  row(nv), row(nv), _const_spec(decay.shape), _const_spec(xi.shape), _const_spec(zeta.shape),
                  _const_spec(gch.shape), _const_spec((1, nv)), _const_spec((1, nv)),
                  _const_spec((nv, D))],
        out_specs=row(D),
        out_shape=jax.ShapeDtypeStruct((T, D), F32),
        scratch_shapes=[pltpu.VMEM((RET_HEADS, RET_QK, RET_V), F32), pltpu.VMEM((ts, nv), BF16)],
        compiler_params=_params("parallel", "arbitrary"),
        name="retention",
    )(x, q, k, v, gate, decay, xi, zeta, gch, gn_g.reshape(1, nv), gn_b.reshape(1, nv), w_out)


def _retention_constants():
    H, C = RET_HEADS, RET_BLOCK
    log_g = jnp.log1p(-jnp.exp2(-5.0 - jnp.arange(H, dtype=F32)))
    idx = jnp.arange(C, dtype=F32)
    diff = idx[:, None] - idx[None, :]
    decay = jnp.where(diff >= 0, jnp.exp(log_g[:, None, None] * jnp.maximum(diff, 0.0)), 0.0)
    xi = jnp.exp(log_g[:, None] * (idx + 1.0))[:, :, None]
    zeta = jnp.exp(log_g[:, None] * (C - 1.0 - idx))[:, None, :]
    gch = jnp.exp(log_g * C)[:, None, None]
    return decay, xi, zeta, gch


def _swap_halves(w):
    half = w.shape[-1] // 2
    return jnp.concatenate([w[..., half:], w[..., :half]], axis=-1)


def _mla_weights(w_in, w_q_b, w_kv_b, q_nope_norm, k_nope_norm, q_rope_norm, k_rope_norm):
    D = w_in.shape[0]
    pad = MLA_HEAD_PAD - QK_NOPE - QK_ROPE
    z = lambda *s: jnp.zeros(s, F32)
    base = 2 * CONV_DIM + Q_LORA + KV_LORA
    w_kr = w_in[:, base:base + QK_ROPE]
    grp = lambda w: jnp.concatenate([z(D, QK_NOPE), w, z(D, pad)], axis=1)
    w_in_p = jnp.concatenate([w_in[:, :base], grp(w_kr), grp(_swap_halves(w_kr))], axis=1)

    wq = w_q_b.reshape(Q_LORA, MLA_HEADS, QK_NOPE + QK_ROPE)
    wq_nope, wq_rope = wq[..., :QK_NOPE], wq[..., QK_NOPE:]
    zq = z(Q_LORA, MLA_HEADS, pad)
    wq_p = jnp.concatenate([wq_nope, wq_rope, zq], axis=-1).reshape(Q_LORA, -1)
    wq_sw = jnp.concatenate([jnp.zeros_like(wq_nope), _swap_halves(wq_rope), zq], axis=-1).reshape(Q_LORA, -1)
    wq_all = jnp.concatenate([wq_p, wq_sw], axis=1)

    wkv = w_kv_b.reshape(KV_LORA, MLA_HEADS, QK_NOPE + V_HEAD)
    wk_p = jnp.concatenate([wkv[..., :QK_NOPE], z(KV_LORA, MLA_HEADS, MLA_HEAD_PAD - QK_NOPE)],
                           axis=-1).reshape(KV_LORA, -1)
    wv_p = wkv[..., QK_NOPE:].reshape(KV_LORA, -1)

    lane = jnp.arange(MLA_HEAD_PAD)
    in_nope = lane < QK_NOPE
    in_rope = (lane >= QK_NOPE) & (lane < QK_NOPE + QK_ROPE)
    seg = (jnp.where(in_nope[:, None] & in_nope[None, :], 1.0 / QK_NOPE, 0.0)
           + jnp.where(in_rope[:, None] & in_rope[None, :], 1.0 / QK_ROPE, 0.0))
    vec = lambda nope, rope: jnp.concatenate([nope, rope, z(pad)]).reshape(1, MLA_HEAD_PAD)
    pair = lambda v: jnp.concatenate([v, v], axis=1)
    gq = pair(vec(q_nope_norm, q_rope_norm))
    gqs = pair(vec(jnp.zeros_like(q_nope_norm), _swap_halves(q_rope_norm)))
    gk = pair(vec(k_nope_norm, z(QK_ROPE)))
    gkr = vec(z(QK_NOPE), k_rope_norm)
    gkrs = vec(z(QK_NOPE), _swap_halves(k_rope_norm))
    zs = jnp.zeros_like(seg)
    seg2 = jnp.concatenate([jnp.concatenate([seg, zs], axis=1), jnp.concatenate([zs, seg], axis=1)], axis=0)
    return (w_in_p.astype(BF16), wq_all.astype(BF16), wk_p.astype(BF16), wv_p.astype(BF16),
            seg2.astype(BF16), gq, gqs, gk, gkr, gkrs)


def kernel(x, mem, positions, ffn1_norm, ffn1_w_gate, ffn1_w_up, ffn1_w_down, ffn2_norm, ffn2_w_gate, ffn2_w_up, ffn2_w_down, mix_norm, xattn_norm, mem_norm, xattn_wq, xattn_wk, xattn_wv, xattn_wo, xattn_q_norm, xattn_k_norm, ev_w_in, ev_conv_w, ev_conv_b, ev_conv_ln_g, ev_conv_ln_b, ev_q_a_norm, ev_w_q_b, ev_kv_a_norm, ev_w_kv_b, ev_q_nope_norm, ev_k_nope_norm, ev_q_rope_norm, ev_k_rope_norm, ev_w_out, od_w_in, od_gn_g, od_gn_b, od_w_out):
    B, S, D = x.shape
    depth = ffn1_norm.shape[0]
    tm = min(TOKEN_TILE, S)
    tf = min(FFN_TOKEN_TILE, B * S)
    te = min(FFN_TOKEN_TILE, S)
    assert S % tm == 0 and (B * S) % tf == 0 and S % min(tf, S) == 0 and S % te == 0 and te % tm == 0
    assert tm % RET_BLOCK == 0 and tm % CONV_HALO == 0 and te % CONV_ROWS == 0
    bf = lambda w: w.astype(BF16)

    h = x.reshape(B * S, D)
    cos_r, sin_r, cm, sm = _rope_tables(positions, math.gcd(B * S, ROPE_TOKEN_TILE))
    decay, xi, zeta, gch = _retention_constants()

    for l in range(depth):
        h = _ffn(h, ffn1_norm[l], bf(ffn1_w_gate[l]), bf(ffn1_w_up[l]), bf(ffn1_w_down[l]), tf)
        if l % 2 == 0:
            e = l // 2
            (w_in_p, wq_p, wk_p, wv_p, seg, gq, gqs, gk, gkr, gkrs) = _mla_weights(
                ev_w_in[e], ev_w_q_b[e], ev_w_kv_b[e], ev_q_nope_norm[e], ev_k_nope_norm[e],
                ev_q_rope_norm[e], ev_k_rope_norm[e])
            a, q, k, v = _even_pre(h, mix_norm[l], w_in_p, cm, sm, ev_q_a_norm[e].reshape(1, -1), wq_p,
                                   ev_kv_a_norm[e].reshape(1, -1), wk_p, wv_p, seg, gq, gqs, gk, gkr,
                                   gkrs, te, tm)
            m = _mla_attention(q, k, v, B, S, tm)
            h = _even_post(h, a, m, ev_conv_w[e], ev_conv_b[e], ev_conv_ln_g[e], ev_conv_ln_b[e],
                           bf(ev_w_out[e][:CONV_DIM]), bf(ev_w_out[e][CONV_DIM:]), te, S)
        else:
            o = l // 2
            q, k, v, gate = _odd_pre(h, mix_norm[l], bf(od_w_in[o]), cos_r, sin_r, tm)
            h = _retention(h, q, k, v, gate, decay, xi, zeta, gch, od_gn_g[o], od_gn_b[o],
                           bf(od_w_out[o]), B, S, tm)
        kt, vw = _memkv(mem, mem_norm[l], bf(xattn_wk[l]), bf(xattn_wv[l]), bf(xattn_wo[l]), xattn_k_norm[l])
        h = _xattn(h, xattn_norm[l], bf(xattn_wq[l]), xattn_q_norm[l], kt, vw, min(tf, S), S)
        h = _ffn(h, ffn2_norm[l], bf(ffn2_w_gate[l]), bf(ffn2_w_up[l]), bf(ffn2_w_down[l]), tf)
    return h.reshape(B, S, D)
```

```python
import functools
import math

import jax
import jax.numpy as jnp
from jax import lax
from jax.experimental import pallas as pl
from jax.experimental.pallas import tpu as pltpu

F32 = jnp.float32
BF16 = jnp.bfloat16

EPS = 1e-6
ROPE_THETA = 10000.0

CONV_DIM = 512
CONV_WIDTH = 31
MLA_HEADS = 8
Q_LORA = 256
KV_LORA = 128
QK_NOPE = 64
QK_ROPE = 32
V_HEAD = 64
RET_HEADS = 4
RET_QK = 256
RET_V = 512
X_HEADS = 4

V7X_LANES = 128
V7X_MXU_DIM = 256
V7X_VMEM_LIMIT_BYTES = 56 * 1024 * 1024

TOKEN_TILE = 512
ROPE_TOKEN_TILE = 2048
FFN_TOKEN_TILE = 1024
RET_BLOCK = 256
CONV_HALO = 32
MLA_HEAD_PAD = 128


def _params(*sem):
    return pltpu.CompilerParams(dimension_semantics=sem, vmem_limit_bytes=V7X_VMEM_LIMIT_BYTES)


def _const_spec(shape):
    nd = len(shape)
    return pl.BlockSpec(shape, lambda *_: (0,) * nd, pipeline_mode=pl.Buffered(1))


def _rms(x, gain):
    return x * lax.rsqrt(jnp.mean(x * x, axis=-1, keepdims=True) + EPS) * gain


def _silu(x):
    return x / (1.0 + jnp.exp(-x))


def _dot(a, b):
    return jnp.dot(a, b, preferred_element_type=F32)


def _select_lanes(t, sel):
    hi = t.astype(BF16)
    rest = t - hi.astype(F32)
    mid = rest.astype(BF16)
    lo = (rest - mid.astype(F32)).astype(BF16)
    return _dot(hi, sel) + _dot(mid, sel) + _dot(lo, sel)


def _rope_kernel(pos_ref, inv_ref, selc_ref, sels_ref, one_ref, cos_ref, sin_ref, cm_ref, sm_ref):
    ang = pos_ref[...].astype(F32) * inv_ref[...]
    cos = jnp.cos(ang)
    sin = jnp.sin(ang)
    cos_ref[...] = cos
    sin_ref[...] = sin
    cm_ref[...] = _select_lanes(cos, selc_ref[...]) + one_ref[...]
    sm_ref[...] = _select_lanes(sin, sels_ref[...])


def _rope_tables(positions, tm):
    T = positions.size
    inv = (ROPE_THETA ** (-jnp.arange(0, RET_QK, 2, dtype=F32) / RET_QK)).reshape(1, RET_QK // 2)
    assert RET_QK % QK_ROPE == 0 and RET_QK // 2 == V7X_LANES
    half = QK_ROPE // 2
    src = jnp.arange(half) * (RET_QK // QK_ROPE)
    lane = jnp.arange(V7X_LANES)
    hit = lambda dst0: (lane[:, None] == src[None, :]).astype(F32) @ \
        (jnp.arange(half)[:, None] + dst0 == lane[None, :]).astype(F32)
    first, second = hit(QK_NOPE), hit(QK_NOPE + half)
    sel_cos = (first + second).astype(BF16)
    sel_sin = (second - first).astype(BF16)
    one = ((lane < QK_NOPE) | (lane >= QK_NOPE + QK_ROPE)).astype(F32).reshape(1, V7X_LANES)
    row = pl.BlockSpec((tm, V7X_LANES), lambda i: (i, 0))
    mat = _const_spec((V7X_LANES, V7X_LANES))
    out = jax.ShapeDtypeStruct((T, V7X_LANES), F32)
    return pl.pallas_call(
        _rope_kernel,
        grid=(T // tm,),
        in_specs=[pl.BlockSpec((tm, 1), lambda i: (i, 0)), _const_spec((1, V7X_LANES)), mat, mat,
                  _const_spec((1, V7X_LANES))],
        out_specs=[row, row, row, row],
        out_shape=[out, out, out, out],
        compiler_params=_params("parallel"),
        name="rope_tables",
    )(positions.reshape(T, 1), inv, sel_cos, sel_sin, one)


def _ff_chunks(d_ff, width):
    edges = list(range(0, d_ff, width)) + [d_ff]
    return [(a, b) for a, b in zip(edges[:-1], edges[1:])]


def _ffn_kernel(x_ref, g_ref, wg_ref, wu_ref, wd_ref, o_ref, h_ref):
    x = x_ref[...]
    xn = _rms(x, g_ref[...]).astype(BF16)
    for a, b in _ff_chunks(h_ref.shape[1], 2 * V7X_MXU_DIM):
        gate = _dot(xn, wg_ref[:, a:b])
        up = _dot(xn, wu_ref[:, a:b])
        h_ref[:, a:b] = (_silu(gate) * up).astype(BF16)
    o_ref[...] = x + 0.5 * _dot(h_ref[...], wd_ref[...])


def _ffn(x, gain, w_gate, w_up, w_down, tm):
    T, D = x.shape
    d_ff = w_gate.shape[1]
    row = pl.BlockSpec((tm, D), lambda i: (i, 0))
    return pl.pallas_call(
        _ffn_kernel,
        grid=(T // tm,),
        in_specs=[row, _const_spec((1, D)), _const_spec((D, d_ff)), _const_spec((D, d_ff)),
                  _const_spec((d_ff, D))],
        out_specs=row,
        out_shape=jax.ShapeDtypeStruct((T, D), F32),
        scratch_shapes=[pltpu.VMEM((tm, d_ff), BF16)],
        compiler_params=_params("parallel"),
        name="ffn",
    )(x, gain.reshape(1, D), w_gate, w_up, w_down)


def _memkv_kernel(mem_ref, g_ref, wk_ref, wv_ref, wo_ref, kn_ref, kt_ref, vw_ref):
    M = mem_ref.shape[0]
    mn = _rms(mem_ref[...], g_ref[...]).astype(BF16)
    k = _dot(mn, wk_ref[...])
    v = _dot(mn, wv_ref[...]).astype(BF16)
    hd = kn_ref.shape[1]
    for h in range(X_HEADS):
        sl = slice(h * hd, (h + 1) * hd)
        kh = _rms(k[:, sl], kn_ref[...])
        kt_ref[0, sl, :] = kh.T.astype(BF16)
        vw_ref[0, h * M:(h + 1) * M, :] = _dot(v[:, sl], wo_ref[sl, :]).astype(BF16)


def _memkv(mem, gain, wk, wv, wo, k_norm):
    B, M, D = mem.shape
    hd = D // X_HEADS
    return pl.pallas_call(
        _memkv_kernel,
        grid=(B,),
        in_specs=[pl.BlockSpec((M, D), lambda b: (b, 0)), _const_spec((1, D)),
                  _const_spec((D, D)), _const_spec((D, D)), _const_spec((D, D)), _const_spec((1, hd))],
        out_specs=[pl.BlockSpec((1, D, M), lambda b: (b, 0, 0)),
                   pl.BlockSpec((1, X_HEADS * M, D), lambda b: (b, 0, 0))],
        out_shape=[jax.ShapeDtypeStruct((B, D, M), BF16), jax.ShapeDtypeStruct((B, X_HEADS * M, D), BF16)],
        compiler_params=_params("parallel"),
        name="xattn_memkv",
    )(mem.reshape(B * M, D), gain.reshape(1, D), wk, wv, wo, k_norm.reshape(1, hd))


def _xattn_kernel(x_ref, g_ref, wq_ref, qn_ref, kt_ref, vw_ref, o_ref, p_ref):
    x = x_ref[...]
    xn = _rms(x, g_ref[...]).astype(BF16)
    q = _dot(xn, wq_ref[...])
    hd = qn_ref.shape[1]
    M = kt_ref.shape[2]
    scale = hd ** -0.5
    for h in range(X_HEADS):
        sl = slice(h * hd, (h + 1) * hd)
        qh = (_rms(q[:, sl], qn_ref[...]) * scale).astype(BF16)
        s = _dot(qh, kt_ref[0, sl, :])
        p = jnp.exp(s - jnp.max(s, axis=-1, keepdims=True))
        p_ref[:, h * M:(h + 1) * M] = (p * (1.0 / jnp.sum(p, axis=-1, keepdims=True))).astype(BF16)
    o_ref[...] = x + _dot(p_ref[...], vw_ref[0])


def _xattn(x, gain, wq, q_norm, kt, vw, tm, seq):
    T, D = x.shape
    M = kt.shape[2]
    hd = D // X_HEADS
    per_seq = seq // tm
    row = pl.BlockSpec((tm, D), lambda i: (i, 0))
    return pl.pallas_call(
        _xattn_kernel,
        grid=(T // tm,),
        in_specs=[row, _const_spec((1, D)), _const_spec((D, D)), _const_spec((1, hd)),
                  pl.BlockSpec((1, D, M), lambda i: (i // per_seq, 0, 0)),
                  pl.BlockSpec((1, X_HEADS * M, D), lambda i: (i // per_seq, 0, 0))],
        out_specs=row,
        out_shape=jax.ShapeDtypeStruct((T, D), F32),
        scratch_shapes=[pltpu.VMEM((tm, X_HEADS * M), BF16)],
        compiler_params=_params("parallel"),
        name="xattn",
    )(x, gain.reshape(1, D), wq, q_norm.reshape(1, hd), kt, vw)


def _even_pre_kernel(x_ref, g_ref, win_ref, cm_ref, sm_ref, qa_ref, wq_ref, kva_ref, wk_ref, wv_ref,
                     seg_ref, gq_ref, gqs_ref, gk_ref, gkr_ref, gkrs_ref,
                     a_ref, qt_ref, k_ref, vt_ref):
    xn = _rms(x_ref[...], g_ref[...]).astype(BF16)
    c = CONV_DIM
    glu = _dot(xn, win_ref[:, 0:2 * c])
    a_ref[...] = glu[:, :c] / (1.0 + jnp.exp(-glu[:, c:]))
    hp = MLA_HEAD_PAD
    z = _dot(xn, win_ref[:, 2 * c:])
    zq = z[:, :Q_LORA]
    zkv = z[:, Q_LORA:Q_LORA + KV_LORA]
    zkr = z[:, Q_LORA + KV_LORA:Q_LORA + KV_LORA + hp]
    zkr_sw = z[:, Q_LORA + KV_LORA + hp:]
    cm = cm_ref[...]
    sm = sm_ref[...]

    rs_kr = lax.rsqrt(jnp.sum(zkr * zkr, axis=-1, keepdims=True) * (1.0 / QK_ROPE) + EPS)
    k_rope = (zkr * rs_kr * gkr_ref[...]) * cm + (zkr_sw * rs_kr * gkrs_ref[...]) * sm

    zq_n = _rms(zq, qa_ref[...]).astype(BF16)
    zkv_n = _rms(zkv, kva_ref[...]).astype(BF16)
    tq = qt_ref.shape[2]
    blocks = [slice(r * tq, (r + 1) * tq) for r in range(qt_ref.shape[0])]
    v = _dot(zkv_n, wv_ref[...])
    for r, rows in enumerate(blocks):
        vt_ref[r] = v[rows].T.astype(BF16)
    pair = lambda t: jnp.concatenate([t, t], axis=1)
    k_rope2 = pair(k_rope)
    q_cos = pair(cm) * (gq_ref[...] * MLA_Q_SCALE)
    q_sin = pair(sm) * (gqs_ref[...] * MLA_Q_SCALE)
    pw = 2 * hp
    nq = MLA_HEADS * hp
    seg = seg_ref[...]
    for h in range(MLA_HEADS // 2):
        sl = slice(h * pw, (h + 1) * pw)
        qh = _dot(zq_n, wq_ref[:, sl])
        qh_sw = _dot(zq_n, wq_ref[:, nq + h * pw:nq + (h + 1) * pw])
        rs = lax.rsqrt(_dot((qh * qh).astype(BF16), seg) + EPS)
        qr = rs * (qh * q_cos + qh_sw * q_sin)
        for r, rows in enumerate(blocks):
            qt_ref[r, sl, :] = qr[rows].T.astype(BF16)
        kh = _dot(zkv_n, wk_ref[:, sl])
        rk = lax.rsqrt(_dot((kh * kh).astype(BF16), seg) + EPS)
        k_ref[:, sl] = (kh * rk * gk_ref[...] + k_rope2).astype(BF16)


def _even_pre(x, gain, w_in_p, cm, sm, qa, wq_p, kva, wk_p, wv_p, seg, gq, gqs, gk, gkr, gkrs, tm, tq):
    T, D = x.shape
    row = lambda w: pl.BlockSpec((tm, w), lambda i: (i, 0))
    hq, hv = MLA_HEADS * MLA_HEAD_PAD, MLA_HEADS * V_HEAD
    consts = [gain.reshape(1, D), w_in_p]
    tables = [cm, sm]
    rest = [qa, wq_p, kva, wk_p, wv_p, seg, gq, gqs, gk, gkr, gkrs]
    return pl.pallas_call(
        _even_pre_kernel,
        grid=(T // tm,),
        in_specs=[row(D)] + [_const_spec(a.shape) for a in consts] + [row(V7X_LANES)] * 2
                 + [_const_spec(a.shape) for a in rest],
        out_specs=[row(CONV_DIM), pl.BlockSpec((tm // tq, hq, tq), lambda i: (i, 0, 0)), row(hq),
                   pl.BlockSpec((tm // tq, hv, tq), lambda i: (i, 0, 0))],
        out_shape=[jax.ShapeDtypeStruct((T, CONV_DIM), F32), jax.ShapeDtypeStruct((T // tq, hq, tq), BF16),
                   jax.ShapeDtypeStruct((T, hq), BF16), jax.ShapeDtypeStruct((T // tq, hv, tq), BF16)],
        compiler_params=_params("parallel"),
        name="even_pre",
    )(x, *consts, *tables, *rest)


MLA_HEADS_PER_STEP = 8
MLA_Q_SCALE = (QK_NOPE + QK_ROPE) ** -0.5 * math.log2(math.e)


MLA_SUM_ROWS = 16
MLA_STAGE_SLOTS = 3


def _mla_kernel(qt_ref, k_ref, vt_ref, o_ref, m_ref, acc_ref, s_ref):
    qi = pl.program_id(2)
    tq = qt_ref.shape[2]
    hp = MLA_HEAD_PAD
    ones = jnp.ones((MLA_SUM_ROWS, tq), BF16)

    m_ref[...] = jnp.full(m_ref.shape, -jnp.inf, F32)
    acc_ref[...] = jnp.zeros(acc_ref.shape, F32)

    def block(j, masked):
        rows = pl.ds(pl.multiple_of(j * tq, tq), tq)
        slots = s_ref.shape[0]

        def stage_scores(h):
            s_ref[h % slots] = _dot(k_ref[rows, h * hp:(h + 1) * hp], qt_ref[0, h * hp:(h + 1) * hp, :])

        def softmax_update(h):
            s = s_ref[h % slots]
            if masked:
                key = lax.broadcasted_iota(jnp.int32, s.shape, 0)
                qry = lax.broadcasted_iota(jnp.int32, s.shape, 1)
                s = jnp.where(key <= qry, s, -jnp.inf)
            m_old = m_ref[h]
            m_new = jnp.maximum(m_old, jnp.max(s, axis=0, keepdims=True))
            p = jnp.exp2(s - m_new).astype(BF16)
            v1 = jnp.concatenate([vt_ref[j, h * V_HEAD:(h + 1) * V_HEAD, :], ones], axis=0)
            acc_ref[h] = jnp.exp2(m_old - m_new) * acc_ref[h] + _dot(v1, p)
            m_ref[h] = m_new

        for h in range(MLA_HEADS_PER_STEP + slots):
            if h >= slots:
                softmax_update(h - slots)
            if h < MLA_HEADS_PER_STEP:
                stage_scores(h)

    def body(j, carry):
        block(j, False)
        return carry

    lax.fori_loop(0, qi, body, 0)
    block(qi, True)
    out = [acc_ref[h, 0:V_HEAD, :] / acc_ref[h, V_HEAD:V_HEAD + 1, :] for h in range(MLA_HEADS_PER_STEP)]
    o_ref[...] = jnp.concatenate(out, axis=0).T.astype(BF16)


def _mla_attention(qt, k, vt, batch, seq, tq):
    T = k.shape[0]
    hs = MLA_HEADS_PER_STEP
    nq = seq // tq
    qw, vw = hs * MLA_HEAD_PAD, hs * V_HEAD
    return pl.pallas_call(
        _mla_kernel,
        grid=(batch, MLA_HEADS // hs, nq),
        in_specs=[pl.BlockSpec((1, qw, tq), lambda b, h, i: (b * nq + i, h, 0)),
                  pl.BlockSpec((seq, qw), lambda b, h, i: (b, h)),
                  pl.BlockSpec((nq, vw, tq), lambda b, h, i: (b, h, 0))],
        out_specs=pl.BlockSpec((tq, vw), lambda b, h, i: (b * nq + i, h)),
        out_shape=jax.ShapeDtypeStruct((T, MLA_HEADS * V_HEAD), BF16),
        scratch_shapes=[pltpu.VMEM((hs, 1, tq), F32), pltpu.VMEM((hs, V_HEAD + MLA_SUM_ROWS, tq), F32),
                        pltpu.VMEM((MLA_STAGE_SLOTS, tq, tq), F32)],
        compiler_params=_params("parallel", "parallel", "arbitrary"),
        name="mla_attention",
    )(qt, k, vt)


CONV_ROWS = 128
V7X_SUBLANES = 8


def _causal_conv(win_ref, cw_ref, ts):
    off = CONV_HALO - (CONV_WIDTH - 1)
    sub = V7X_SUBLANES
    lane_blocks = []
    for lb in range(win_ref.shape[1] // V7X_LANES):
        lanes = slice(lb * V7X_LANES, (lb + 1) * V7X_LANES)
        row_blocks = []
        for base in range(0, ts, CONV_ROWS):
            y = None
            for r in range(sub):
                n = CONV_ROWS + (sub if r else 0)
                z = None
                for q in range(-(-(off + CONV_WIDTH) // sub)):
                    j = sub * q + r - off
                    if 0 <= j < CONV_WIDTH:
                        lo = base + sub * q
                        term = cw_ref[j:j + 1, lanes] * win_ref[lo:lo + n, lanes]
                        z = term if z is None else z + term
                z = z[r:r + CONV_ROWS]
                y = z if y is None else y + z
            row_blocks.append(y)
        lane_blocks.append(jnp.concatenate(row_blocks, axis=0))
    return jnp.concatenate(lane_blocks, axis=1)


def _even_post_kernel(x_ref, halo_ref, a_ref, m_ref, cw_ref, cb_ref, lg_ref, lb_ref, wa_ref, wm_ref,
                      o_ref, win_ref, *, per_seq):
    ts = a_ref.shape[0]
    first = (pl.program_id(0) % per_seq) == 0
    win_ref[0:CONV_HALO, :] = jnp.where(first, 0.0, halo_ref[...])
    win_ref[CONV_HALO:, :] = a_ref[...]
    y = _causal_conv(win_ref, cw_ref, ts) + cb_ref[...]
    mu = jnp.mean(y, axis=-1, keepdims=True)
    yc = y - mu
    var = jnp.mean(yc * yc, axis=-1, keepdims=True)
    act = _silu(yc * lax.rsqrt(var + EPS) * lg_ref[...] + lb_ref[...]).astype(BF16)
    o_ref[...] = x_ref[...] + _dot(act, wa_ref[...]) + _dot(m_ref[...], wm_ref[...])


def _even_post(x, a, m, conv_w, conv_b, ln_g, ln_b, w_out_a, w_out_m, ts, seq):
    T, D = x.shape
    C = a.shape[1]
    per_seq = seq // ts
    halo_blocks = ts // CONV_HALO
    row = lambda w: pl.BlockSpec((ts, w), lambda i: (i, 0))
    return pl.pallas_call(
        functools.partial(_even_post_kernel, per_seq=per_seq),
        grid=(T // ts,),
        in_specs=[row(D),
                  pl.BlockSpec((CONV_HALO, C), lambda i: (jnp.maximum(i * halo_blocks - 1, 0), 0)),
                  row(C), row(C), _const_spec((CONV_WIDTH, C)), _const_spec((1, C)),
                  _const_spec((1, C)), _const_spec((1, C)), _const_spec((C, D)), _const_spec((C, D))],
        out_specs=row(D),
        out_shape=jax.ShapeDtypeStruct((T, D), F32),
        scratch_shapes=[pltpu.VMEM((CONV_HALO + ts, C), F32)],
        compiler_params=_params("parallel"),
        name="even_post",
    )(x, a, a, m, conv_w, conv_b.reshape(1, C), ln_g.reshape(1, C), ln_b.reshape(1, C),
      w_out_a, w_out_m)


def _odd_pre_kernel(x_ref, g_ref, w_ref, cos_ref, sin_ref, q_ref, kt_ref, v_ref, gate_ref):
    xn = _rms(x_ref[...], g_ref[...]).astype(BF16)
    cos = cos_ref[...]
    sin = sin_ref[...]
    half = RET_QK // 2
    nqk = RET_HEADS * RET_QK
    k_scale = RET_QK ** -0.5
    nv = RET_HEADS * RET_V
    for h in range(RET_HEADS):
        sl = slice(h * RET_V, (h + 1) * RET_V)
        gate_ref[:, sl] = _silu(_dot(xn, w_ref[:, 2 * nqk + nv + h * RET_V:2 * nqk + nv + (h + 1) * RET_V]))
    for h in range(RET_HEADS):
        lo = h * RET_QK
        z = _dot(xn, w_ref[:, lo:lo + RET_QK])
        x1, x2 = z[:, :half], z[:, half:]
        q_ref[:, lo:lo + half] = (x1 * cos - x2 * sin).astype(BF16)
        q_ref[:, lo + half:lo + RET_QK] = (x1 * sin + x2 * cos).astype(BF16)
        z = _dot(xn, w_ref[:, nqk + lo:nqk + lo + RET_QK])
        x1, x2 = z[:, :half], z[:, half:]
        k = jnp.concatenate([x1 * cos - x2 * sin, x1 * sin + x2 * cos], axis=1) * k_scale
        kt_ref[0, lo:lo + RET_QK, :] = k.T.astype(BF16)
    for h in range(RET_HEADS):
        sl = slice(h * RET_V, (h + 1) * RET_V)
        v_ref[:, sl] = _dot(xn, w_ref[:, 2 * nqk + h * RET_V:2 * nqk + (h + 1) * RET_V]).astype(BF16)


def _odd_pre(x, gain, w_in, cos_r, sin_r, tm):
    T, D = x.shape
    nqk, nv = RET_HEADS * RET_QK, RET_HEADS * RET_V
    row = lambda w: pl.BlockSpec((tm, w), lambda i: (i, 0))
    return pl.pallas_call(
        _odd_pre_kernel,
        grid=(T // tm,),
        in_specs=[row(D), _const_spec((1, D)), _const_spec(w_in.shape), row(V7X_LANES), row(V7X_LANES)],
        out_specs=[row(nqk), pl.BlockSpec((1, nqk, tm), lambda i: (i, 0, 0)), row(nv), row(nv)],
        out_shape=[jax.ShapeDtypeStruct((T, nqk), BF16), jax.ShapeDtypeStruct((T // tm, nqk, tm), BF16),
                   jax.ShapeDtypeStruct((T, nv), BF16), jax.ShapeDtypeStruct((T, nv), F32)],
        compiler_params=_params("parallel"),
        name="odd_pre",
    )(x, gain.reshape(1, D), w_in, cos_r, sin_r)


def _retention_kernel(x_ref, q_ref, kt_ref, v_ref, gate_ref, decay_ref, xi_ref, zeta_ref, gch_ref,
                      gng_ref, gnb_ref, wo_ref, o_ref, state_ref, y_ref):
    @pl.when(pl.program_id(1) == 0)
    def _():
        state_ref[...] = jnp.zeros(state_ref.shape, F32)

    ts = x_ref.shape[0]
    C = RET_BLOCK
    for c in range(ts // C):
        rows = slice(c * C, (c + 1) * C)
        for h in range(RET_HEADS):
            qk = slice(h * RET_QK, (h + 1) * RET_QK)
            vv = slice(h * RET_V, (h + 1) * RET_V)
            qc = q_ref[rows, qk]
            kt = kt_ref[0, qk, rows]
            vc = v_ref[rows, vv]
            state = state_ref[h]
            scores = (_dot(qc, kt) * decay_ref[h]).astype(BF16)
            out = _dot(scores, vc) + _dot(qc, state.astype(BF16)) * xi_ref[h]
            kz = (kt.astype(F32) * zeta_ref[h]).astype(BF16)
            state_ref[h] = state * gch_ref[h] + _dot(kz, vc)
            mu = jnp.mean(out, axis=-1, keepdims=True)
            oc = out - mu
            var = jnp.mean(oc * oc, axis=-1, keepdims=True)
            gn = oc * lax.rsqrt(var + EPS) * gng_ref[:, vv] + gnb_ref[:, vv]
            y_ref[rows, vv] = (gate_ref[rows, vv] * gn).astype(BF16)
    o_ref[...] = x_ref[...] + _dot(y_ref[...], wo_ref[...])


def _retention(x, q, k, v, gate, decay, xi, zeta, gch, gn_g, gn_b, w_out, batch, seq, ts):
    T, D = x.shape
    nqk, nv = RET_HEADS * RET_QK, RET_HEADS * RET_V
    per_seq = seq // ts
    row = lambda w: pl.BlockSpec((ts, w), lambda b, i: (b * per_seq + i, 0))
    return pl.pallas_call(
        _retention_kernel,
        grid=(batch, per_seq),
        in_specs=[row(D), row(nqk), pl.BlockSpec((1, nqk, ts), lambda b, i: (b * per_seq + i, 0, 0)),
                  row(nv), row(nv), _const_spec(decay.shape), _const_spec(xi.shape), _const_spec(zeta.shape),
                  _const_spec(gch.shape), _const_spec((1, nv)), _const_spec((1, nv)),
                  _const_spec((nv, D))],
        out_specs=row(D),
        out_shape=jax.ShapeDtypeStruct((T, D), F32),
        scratch_shapes=[pltpu.VMEM((RET_HEADS, RET_QK, RET_V), F32), pltpu.VMEM((ts, nv), BF16)],
        compiler_params=_params("parallel", "arbitrary"),
        name="retention",
    )(x, q, k, v, gate, decay, xi, zeta, gch, gn_g.reshape(1, nv), gn_b.reshape(1, nv), w_out)


def _retention_constants():
    H, C = RET_HEADS, RET_BLOCK
    log_g = jnp.log1p(-jnp.exp2(-5.0 - jnp.arange(H, dtype=F32)))
    idx = jnp.arange(C, dtype=F32)
    diff = idx[:, None] - idx[None, :]
    decay = jnp.where(diff >= 0, jnp.exp(log_g[:, None, None] * jnp.maximum(diff, 0.0)), 0.0)
    xi = jnp.exp(log_g[:, None] * (idx + 1.0))[:, :, None]
    zeta = jnp.exp(log_g[:, None] * (C - 1.0 - idx))[:, None, :]
    gch = jnp.exp(log_g * C)[:, None, None]
    return decay, xi, zeta, gch


def _swap_halves(w):
    half = w.shape[-1] // 2
    return jnp.concatenate([w[..., half:], w[..., :half]], axis=-1)


def _mla_weights(w_in, w_q_b, w_kv_b, q_nope_norm, k_nope_norm, q_rope_norm, k_rope_norm):
    D = w_in.shape[0]
    pad = MLA_HEAD_PAD - QK_NOPE - QK_ROPE
    z = lambda *s: jnp.zeros(s, F32)
    base = 2 * CONV_DIM + Q_LORA + KV_LORA
    w_kr = w_in[:, base:base + QK_ROPE]
    grp = lambda w: jnp.concatenate([z(D, QK_NOPE), w, z(D, pad)], axis=1)
    w_in_p = jnp.concatenate([w_in[:, :base], grp(w_kr), grp(_swap_halves(w_kr))], axis=1)

    wq = w_q_b.reshape(Q_LORA, MLA_HEADS, QK_NOPE + QK_ROPE)
    wq_nope, wq_rope = wq[..., :QK_NOPE], wq[..., QK_NOPE:]
    zq = z(Q_LORA, MLA_HEADS, pad)
    wq_p = jnp.concatenate([wq_nope, wq_rope, zq], axis=-1).reshape(Q_LORA, -1)
    wq_sw = jnp.concatenate([jnp.zeros_like(wq_nope), _swap_halves(wq_rope), zq], axis=-1).reshape(Q_LORA, -1)
    wq_all = jnp.concatenate([wq_p, wq_sw], axis=1)

    wkv = w_kv_b.reshape(KV_LORA, MLA_HEADS, QK_NOPE + V_HEAD)
    wk_p = jnp.concatenate([wkv[..., :QK_NOPE], z(KV_LORA, MLA_HEADS, MLA_HEAD_PAD - QK_NOPE)],
                           axis=-1).reshape(KV_LORA, -1)
    wv_p = wkv[..., QK_NOPE:].reshape(KV_LORA, -1)

    lane = jnp.arange(MLA_HEAD_PAD)
    in_nope = lane < QK_NOPE
    in_rope = (lane >= QK_NOPE) & (lane < QK_NOPE + QK_ROPE)
    seg = (jnp.where(in_nope[:, None] & in_nope[None, :], 1.0 / QK_NOPE, 0.0)
           + jnp.where(in_rope[:, None] & in_rope[None, :], 1.0 / QK_ROPE, 0.0))
    vec = lambda nope, rope: jnp.concatenate([nope, rope, z(pad)]).reshape(1, MLA_HEAD_PAD)
    pair = lambda v: jnp.concatenate([v, v], axis=1)
    gq = pair(vec(q_nope_norm, q_rope_norm))
    gqs = pair(vec(jnp.zeros_like(q_nope_norm), _swap_halves(q_rope_norm)))
    gk = pair(vec(k_nope_norm, z(QK_ROPE)))
    gkr = vec(z(QK_NOPE), k_rope_norm)
    gkrs = vec(z(QK_NOPE), _swap_halves(k_rope_norm))
    zs = jnp.zeros_like(seg)
    seg2 = jnp.concatenate([jnp.concatenate([seg, zs], axis=1), jnp.concatenate([zs, seg], axis=1)], axis=0)
    return (w_in_p.astype(BF16), wq_all.astype(BF16), wk_p.astype(BF16), wv_p.astype(BF16),
            seg2.astype(BF16), gq, gqs, gk, gkr, gkrs)


def kernel(x, mem, positions, ffn1_norm, ffn1_w_gate, ffn1_w_up, ffn1_w_down, ffn2_norm, ffn2_w_gate, ffn2_w_up, ffn2_w_down, mix_norm, xattn_norm, mem_norm, xattn_wq, xattn_wk, xattn_wv, xattn_wo, xattn_q_norm, xattn_k_norm, ev_w_in, ev_conv_w, ev_conv_b, ev_conv_ln_g, ev_conv_ln_b, ev_q_a_norm, ev_w_q_b, ev_kv_a_norm, ev_w_kv_b, ev_q_nope_norm, ev_k_nope_norm, ev_q_rope_norm, ev_k_rope_norm, ev_w_out, od_w_in, od_gn_g, od_gn_b, od_w_out):
    B, S, D = x.shape
    depth = ffn1_norm.shape[0]
    tm = min(TOKEN_TILE, S)
    tf = min(FFN_TOKEN_TILE, B * S)
    te = min(FFN_TOKEN_TILE, S)
    assert S % tm == 0 and (B * S) % tf == 0 and S % min(tf, S) == 0 and S % te == 0 and te % tm == 0
    assert tm % RET_BLOCK == 0 and tm % CONV_HALO == 0 and te % CONV_ROWS == 0
    bf = lambda w: w.astype(BF16)

    h = x.reshape(B * S, D)
    cos_r, sin_r, cm, sm = _rope_tables(positions, math.gcd(B * S, ROPE_TOKEN_TILE))
    decay, xi, zeta, gch = _retention_constants()

    for l in range(depth):
        h = _ffn(h, ffn1_norm[l], bf(ffn1_w_gate[l]), bf(ffn1_w_up[l]), bf(ffn1_w_down[l]), tf)
        if l % 2 == 0:
            e = l // 2
            (w_in_p, wq_p, wk_p, wv_p, seg, gq, gqs, gk, gkr, gkrs) = _mla_weights(
                ev_w_in[e], ev_w_q_b[e], ev_w_kv_b[e], ev_q_nope_norm[e], ev_k_nope_norm[e],
                ev_q_rope_norm[e], ev_k_rope_norm[e])
            a, q, k, v = _even_pre(h, mix_norm[l], w_in_p, cm, sm, ev_q_a_norm[e].reshape(1, -1), wq_p,
                                   ev_kv_a_norm[e].reshape(1, -1), wk_p, wv_p, seg, gq, gqs, gk, gkr,
                                   gkrs, te, tm)
            m = _mla_attention(q, k, v, B, S, tm)
            h = _even_post(h, a, m, ev_conv_w[e], ev_conv_b[e], ev_conv_ln_g[e], ev_conv_ln_b[e],
                           bf(ev_w_out[e][:CONV_DIM]), bf(ev_w_out[e][CONV_DIM:]), te, S)
        else:
            o = l // 2
            q, k, v, gate = _odd_pre(h, mix_norm[l], bf(od_w_in[o]), cos_r, sin_r, tm)
            h = _retention(h, q, k, v, gate, decay, xi, zeta, gch, od_gn_g[o], od_gn_b[o],
                           bf(od_w_out[o]), B, S, tm)
        kt, vw = _memkv(mem, mem_norm[l], bf(xattn_wk[l]), bf(xattn_wv[l]), bf(xattn_wo[l]), xattn_k_norm[l])
        h = _xattn(h, xattn_norm[l], bf(xattn_wq[l]), xattn_q_norm[l], kt, vw, min(tf, S), S)
        h = _ffn(h, ffn2_norm[l], bf(ffn2_w_gate[l]), bf(ffn2_w_up[l]), bf(ffn2_w_down[l]), tf)
    return h.reshape(B, S, D)
```

```python
import functools
import math
from typing import NamedTuple

import jax
import jax.numpy as jnp
from jax import lax
from jax.experimental import pallas as pl
from jax.experimental.pallas import tpu as pltpu

F32 = jnp.float32
BF16 = jnp.bfloat16

EPS = 1e-6
ROPE_THETA = 10000.0

CONV_DIM = 512
CONV_WIDTH = 31
MLA_HEADS = 8
Q_LORA = 256
KV_LORA = 128
QK_NOPE = 64
QK_ROPE = 32
V_HEAD = 64
RET_HEADS = 4
RET_QK = 256
RET_V = 512
X_HEADS = 4

V7X_LANES = 128
V7X_MXU_DIM = 256
V7X_VMEM_LIMIT_BYTES = 56 * 1024 * 1024

TOKEN_TILE = 512
ROPE_TOKEN_TILE = 2048
FFN_TOKEN_TILE = 1024
RET_BLOCK = 256
CONV_HALO = 32
MLA_HEAD_PAD = 128


def _params(*sem):
    return pltpu.CompilerParams(dimension_semantics=sem, vmem_limit_bytes=V7X_VMEM_LIMIT_BYTES)


def _const_spec(shape):
    nd = len(shape)
    return pl.BlockSpec(shape, lambda *_: (0,) * nd, pipeline_mode=pl.Buffered(1))


class _Layer(NamedTuple):
    stack: jax.Array
    index: int


def _layer_spec(w):
    shape = w.stack.shape[1:]
    nd = len(shape)
    return pl.BlockSpec((None,) + shape, lambda *_: (w.index,) + (0,) * nd, pipeline_mode=pl.Buffered(1))


def _rms(x, gain):
    return x * lax.rsqrt(jnp.mean(x * x, axis=-1, keepdims=True) + EPS) * gain


def _silu(x):
    return x / (1.0 + jnp.exp(-x))


def _dot(a, b):
    return jnp.dot(a, b, preferred_element_type=F32)


def _select_lanes(t, sel):
    hi = t.astype(BF16)
    rest = t - hi.astype(F32)
    mid = rest.astype(BF16)
    lo = (rest - mid.astype(F32)).astype(BF16)
    return _dot(hi, sel) + _dot(mid, sel) + _dot(lo, sel)


def _rope_kernel(pos_ref, inv_ref, selc_ref, sels_ref, one_ref, cos_ref, sin_ref, cm_ref, sm_ref):
    ang = pos_ref[...].astype(F32) * inv_ref[...]
    cos = jnp.cos(ang)
    sin = jnp.sin(ang)
    cos_ref[...] = cos
    sin_ref[...] = sin
    cm_ref[...] = _select_lanes(cos, selc_ref[...]) + one_ref[...]
    sm_ref[...] = _select_lanes(sin, sels_ref[...])


def _rope_tables(positions, tm):
    T = positions.size
    inv = (ROPE_THETA ** (-jnp.arange(0, RET_QK, 2, dtype=F32) / RET_QK)).reshape(1, RET_QK // 2)
    assert RET_QK % QK_ROPE == 0 and RET_QK // 2 == V7X_LANES
    half = QK_ROPE // 2
    src = jnp.arange(half) * (RET_QK // QK_ROPE)
    lane = jnp.arange(V7X_LANES)
    hit = lambda dst0: (lane[:, None] == src[None, :]).astype(F32) @ \
        (jnp.arange(half)[:, None] + dst0 == lane[None, :]).astype(F32)
    first, second = hit(QK_NOPE), hit(QK_NOPE + half)
    sel_cos = (first + second).astype(BF16)
    sel_sin = (second - first).astype(BF16)
    one = ((lane < QK_NOPE) | (lane >= QK_NOPE + QK_ROPE)).astype(F32).reshape(1, V7X_LANES)
    row = pl.BlockSpec((tm, V7X_LANES), lambda i: (i, 0))
    mat = _const_spec((V7X_LANES, V7X_LANES))
    out = jax.ShapeDtypeStruct((T, V7X_LANES), F32)
    return pl.pallas_call(
        _rope_kernel,
        grid=(T // tm,),
        in_specs=[pl.BlockSpec((tm, 1), lambda i: (i, 0)), _const_spec((1, V7X_LANES)), mat, mat,
                  _const_spec((1, V7X_LANES))],
        out_specs=[row, row, row, row],
        out_shape=[out, out, out, out],
        compiler_params=_params("parallel"),
        name="rope_tables",
    )(positions.reshape(T, 1), inv, sel_cos, sel_sin, one)


def _ff_chunks(d_ff, width):
    edges = list(range(0, d_ff, width)) + [d_ff]
    return [(a, b) for a, b in zip(edges[:-1], edges[1:])]


def _ffn_kernel(x_ref, g_ref, wg_ref, wu_ref, wd_ref, o_ref, h_ref):
    x = x_ref[...]
    xn = _rms(x, g_ref[...]).astype(BF16)
    for a, b in _ff_chunks(h_ref.shape[1], 2 * V7X_MXU_DIM):
        gate = _dot(xn, wg_ref[:, a:b])
        up = _dot(xn, wu_ref[:, a:b])
        h_ref[:, a:b] = (_silu(gate) * up).astype(BF16)
    o_ref[...] = x + 0.5 * _dot(h_ref[...], wd_ref[...])


def _ffn(x, gain, w_gate, w_up, w_down, tm):
    T, D = x.shape
    d_ff = w_gate.stack.shape[2]
    row = pl.BlockSpec((tm, D), lambda i: (i, 0))
    return pl.pallas_call(
        _ffn_kernel,
        grid=(T // tm,),
        in_specs=[row, _const_spec((1, D)), _layer_spec(w_gate), _layer_spec(w_up), _layer_spec(w_down)],
        out_specs=row,
        out_shape=jax.ShapeDtypeStruct((T, D), F32),
        scratch_shapes=[pltpu.VMEM((tm, d_ff), BF16)],
        compiler_params=_params("parallel"),
        name="ffn",
    )(x, gain.reshape(1, D), w_gate.stack, w_up.stack, w_down.stack)


def _memkv_kernel(mem_ref, g_ref, wk_ref, wv_ref, wo_ref, kn_ref, kt_ref, vw_ref):
    M = mem_ref.shape[0]
    mn = _rms(mem_ref[...], g_ref[...]).astype(BF16)
    k = _dot(mn, wk_ref[...])
    v = _dot(mn, wv_ref[...]).astype(BF16)
    hd = kn_ref.shape[1]
    for h in range(X_HEADS):
        sl = slice(h * hd, (h + 1) * hd)
        kh = _rms(k[:, sl], kn_ref[...])
        kt_ref[0, sl, :] = kh.T.astype(BF16)
        vw_ref[0, h * M:(h + 1) * M, :] = _dot(v[:, sl], wo_ref[sl, :]).astype(BF16)


def _memkv(mem, gain, wk, wv, wo, k_norm):
    B, M, D = mem.shape
    hd = D // X_HEADS
    return pl.pallas_call(
        _memkv_kernel,
        grid=(B,),
        in_specs=[pl.BlockSpec((M, D), lambda b: (b, 0)), _const_spec((1, D)),
                  _layer_spec(wk), _layer_spec(wv), _layer_spec(wo), _const_spec((1, hd))],
        out_specs=[pl.BlockSpec((1, D, M), lambda b: (b, 0, 0)),
                   pl.BlockSpec((1, X_HEADS * M, D), lambda b: (b, 0, 0))],
        out_shape=[jax.ShapeDtypeStruct((B, D, M), BF16), jax.ShapeDtypeStruct((B, X_HEADS * M, D), BF16)],
        compiler_params=_params("parallel"),
        name="xattn_memkv",
    )(mem.reshape(B * M, D), gain.reshape(1, D), wk.stack, wv.stack, wo.stack, k_norm.reshape(1, hd))


def _xattn_kernel(x_ref, g_ref, wq_ref, qn_ref, kt_ref, vw_ref, o_ref, p_ref):
    x = x_ref[...]
    xn = _rms(x, g_ref[...]).astype(BF16)
    q = _dot(xn, wq_ref[...])
    hd = qn_ref.shape[1]
    M = kt_ref.shape[2]
    scale = hd ** -0.5
    for h in range(X_HEADS):
        sl = slice(h * hd, (h + 1) * hd)
        qh = (_rms(q[:, sl], qn_ref[...]) * scale).astype(BF16)
        s = _dot(qh, kt_ref[0, sl, :])
        p = jnp.exp(s - jnp.max(s, axis=-1, keepdims=True))
        p_ref[:, h * M:(h + 1) * M] = (p * (1.0 / jnp.sum(p, axis=-1, keepdims=True))).astype(BF16)
    o_ref[...] = x + _dot(p_ref[...], vw_ref[0])


def _xattn(x, gain, wq, q_norm, kt, vw, tm, seq):
    T, D = x.shape
    M = kt.shape[2]
    hd = D // X_HEADS
    per_seq = seq // tm
    row = pl.BlockSpec((tm, D), lambda i: (i, 0))
    return pl.pallas_call(
        _xattn_kernel,
        grid=(T // tm,),
        in_specs=[row, _const_spec((1, D)), _layer_spec(wq), _const_spec((1, hd)),
                  pl.BlockSpec((1, D, M), lambda i: (i // per_seq, 0, 0)),
                  pl.BlockSpec((1, X_HEADS * M, D), lambda i: (i // per_seq, 0, 0))],
        out_specs=row,
        out_shape=jax.ShapeDtypeStruct((T, D), F32),
        scratch_shapes=[pltpu.VMEM((tm, X_HEADS * M), BF16)],
        compiler_params=_params("parallel"),
        name="xattn",
    )(x, gain.reshape(1, D), wq.stack, q_norm.reshape(1, hd), kt, vw)


def _even_pre_kernel(x_ref, g_ref, win_ref, cm_ref, sm_ref, qa_ref, wq_ref, kva_ref, wk_ref, wv_ref,
                     seg_ref, gq_ref, gqs_ref, gk_ref, gkr_ref, gkrs_ref,
                     a_ref, qt_ref, k_ref, vt_ref):
    xn = _rms(x_ref[...], g_ref[...]).astype(BF16)
    c = CONV_DIM
    glu = _dot(xn, win_ref[:, 0:2 * c])
    a_ref[...] = glu[:, :c] / (1.0 + jnp.exp(-glu[:, c:]))
    hp = MLA_HEAD_PAD
    z = _dot(xn, win_ref[:, 2 * c:])
    zq = z[:, :Q_LORA]
    zkv = z[:, Q_LORA:Q_LORA + KV_LORA]
    zkr = z[:, Q_LORA + KV_LORA:Q_LORA + KV_LORA + hp]
    zkr_sw = z[:, Q_LORA + KV_LORA + hp:]
    cm = cm_ref[...]
    sm = sm_ref[...]

    rs_kr = lax.rsqrt(jnp.sum(zkr * zkr, axis=-1, keepdims=True) * (1.0 / QK_ROPE) + EPS)
    k_rope = (zkr * rs_kr * gkr_ref[...]) * cm + (zkr_sw * rs_kr * gkrs_ref[...]) * sm

    zq_n = _rms(zq, qa_ref[...]).astype(BF16)
    zkv_n = _rms(zkv, kva_ref[...]).astype(BF16)
    tq = qt_ref.shape[2]
    blocks = [slice(r * tq, (r + 1) * tq) for r in range(qt_ref.shape[0])]
    v = _dot(zkv_n, wv_ref[...])
    for r, rows in enumerate(blocks):
        vt_ref[r] = v[rows].T.astype(BF16)
    pair = lambda t: jnp.concatenate([t, t], axis=1)
    k_rope2 = pair(k_rope)
    q_cos = pair(cm) * (gq_ref[...] * MLA_Q_SCALE)
    q_sin = pair(sm) * (gqs_ref[...] * MLA_Q_SCALE)
    pw = 2 * hp
    nq = MLA_HEADS * hp
    seg = seg_ref[...]
    for h in range(MLA_HEADS // 2):
        sl = slice(h * pw, (h + 1) * pw)
        qh = _dot(zq_n, wq_ref[:, sl])
        qh_sw = _dot(zq_n, wq_ref[:, nq + h * pw:nq + (h + 1) * pw])
        rs = lax.rsqrt(_dot((qh * qh).astype(BF16), seg) + EPS)
        qr = rs * (qh * q_cos + qh_sw * q_sin)
        for r, rows in enumerate(blocks):
            qt_ref[r, sl, :] = qr[rows].T.astype(BF16)
        kh = _dot(zkv_n, wk_ref[:, sl])
        rk = lax.rsqrt(_dot((kh * kh).astype(BF16), seg) + EPS)
        k_ref[:, sl] = (kh * rk * gk_ref[...] + k_rope2).astype(BF16)


def _even_pre(x, gain, w_in_p, cm, sm, qa, wq_p, kva, wk_p, wv_p, seg, gq, gqs, gk, gkr, gkrs, tm, tq):
    T, D = x.shape
    row = lambda w: pl.BlockSpec((tm, w), lambda i: (i, 0))
    hq, hv = MLA_HEADS * MLA_HEAD_PAD, MLA_HEADS * V_HEAD
    consts = [gain.reshape(1, D), w_in_p]
    tables = [cm, sm]
    rest = [qa, wq_p, kva, wk_p, wv_p, seg, gq, gqs, gk, gkr, gkrs]
    return pl.pallas_call(
        _even_pre_kernel,
        grid=(T // tm,),
        in_specs=[row(D)] + [_const_spec(a.shape) for a in consts] + [row(V7X_LANES)] * 2
                 + [_const_spec(a.shape) for a in rest],
        out_specs=[row(CONV_DIM), pl.BlockSpec((tm // tq, hq, tq), lambda i: (i, 0, 0)), row(hq),
                   pl.BlockSpec((tm // tq, hv, tq), lambda i: (i, 0, 0))],
        out_shape=[jax.ShapeDtypeStruct((T, CONV_DIM), F32), jax.ShapeDtypeStruct((T // tq, hq, tq), BF16),
                   jax.ShapeDtypeStruct((T, hq), BF16), jax.ShapeDtypeStruct((T // tq, hv, tq), BF16)],
        compiler_params=_params("parallel"),
        name="even_pre",
    )(x, *consts, *tables, *rest)


MLA_HEADS_PER_STEP = 8
MLA_Q_SCALE = (QK_NOPE + QK_ROPE) ** -0.5 * math.log2(math.e)


MLA_SUM_ROWS = 16
MLA_STAGE_SLOTS = 3


def _mla_kernel(qt_ref, k_ref, vt_ref, o_ref, m_ref, acc_ref, s_ref):
    qi = pl.program_id(2)
    tq = qt_ref.shape[2]
    hp = MLA_HEAD_PAD
    ones = jnp.ones((MLA_SUM_ROWS, tq), BF16)

    m_ref[...] = jnp.full(m_ref.shape, -jnp.inf, F32)
    acc_ref[...] = jnp.zeros(acc_ref.shape, F32)

    def block(j, masked):
        rows = pl.ds(pl.multiple_of(j * tq, tq), tq)
        slots = s_ref.shape[0]

        def stage_scores(h):
            s_ref[h % slots] = _dot(k_ref[rows, h * hp:(h + 1) * hp], qt_ref[0, h * hp:(h + 1) * hp, :])

        def softmax_update(h):
            s = s_ref[h % slots]
            if masked:
                key = lax.broadcasted_iota(jnp.int32, s.shape, 0)
                qry = lax.broadcasted_iota(jnp.int32, s.shape, 1)
                s = jnp.where(key <= qry, s, -jnp.inf)
            m_old = m_ref[h]
            m_new = jnp.maximum(m_old, jnp.max(s, axis=0, keepdims=True))
            p = jnp.exp2(s - m_new).astype(BF16)
            v1 = jnp.concatenate([vt_ref[j, h * V_HEAD:(h + 1) * V_HEAD, :], ones], axis=0)
            acc_ref[h] = jnp.exp2(m_old - m_new) * acc_ref[h] + _dot(v1, p)
            m_ref[h] = m_new

        for h in range(MLA_HEADS_PER_STEP + slots):
            if h >= slots:
                softmax_update(h - slots)
            if h < MLA_HEADS_PER_STEP:
                stage_scores(h)

    def body(j, carry):
        block(j, False)
        return carry

    lax.fori_loop(0, qi, body, 0)
    block(qi, True)
    out = [acc_ref[h, 0:V_HEAD, :] / acc_ref[h, V_HEAD:V_HEAD + 1, :] for h in range(MLA_HEADS_PER_STEP)]
    o_ref[...] = jnp.concatenate(out, axis=0).T.astype(BF16)


def _mla_attention(qt, k, vt, batch, seq, tq):
    T = k.shape[0]
    hs = MLA_HEADS_PER_STEP
    nq = seq // tq
    qw, vw = hs * MLA_HEAD_PAD, hs * V_HEAD
    return pl.pallas_call(
        _mla_kernel,
        grid=(batch, MLA_HEADS // hs, nq),
        in_specs=[pl.BlockSpec((1, qw, tq), lambda b, h, i: (b * nq + i, h, 0)),
                  pl.BlockSpec((seq, qw), lambda b, h, i: (b, h)),
                  pl.BlockSpec((nq, vw, tq), lambda b, h, i: (b, h, 0))],
        out_specs=pl.BlockSpec((tq, vw), lambda b, h, i: (b * nq + i, h)),
        out_shape=jax.ShapeDtypeStruct((T, MLA_HEADS * V_HEAD), BF16),
        scratch_shapes=[pltpu.VMEM((hs, 1, tq), F32), pltpu.VMEM((hs, V_HEAD + MLA_SUM_ROWS, tq), F32),
                        pltpu.VMEM((MLA_STAGE_SLOTS, tq, tq), F32)],
        compiler_params=_params("parallel", "parallel", "arbitrary"),
        name="mla_attention",
    )(qt, k, vt)


CONV_ROWS = 128
V7X_SUBLANES = 8


def _causal_conv(win_ref, cw_ref, ts):
    off = CONV_HALO - (CONV_WIDTH - 1)
    sub = V7X_SUBLANES
    lane_blocks = []
    for lb in range(win_ref.shape[1] // V7X_LANES):
        lanes = slice(lb * V7X_LANES, (lb + 1) * V7X_LANES)
        row_blocks = []
        for base in range(0, ts, CONV_ROWS):
            y = None
            for r in range(sub):
                n = CONV_ROWS + (sub if r else 0)
                z = None
                for q in range(-(-(off + CONV_WIDTH) // sub)):
                    j = sub * q + r - off
                    if 0 <= j < CONV_WIDTH:
                        lo = base + sub * q
                        term = cw_ref[j:j + 1, lanes] * win_ref[lo:lo + n, lanes]
                        z = term if z is None else z + term
                z = z[r:r + CONV_ROWS]
                y = z if y is None else y + z
            row_blocks.append(y)
        lane_blocks.append(jnp.concatenate(row_blocks, axis=0))
    return jnp.concatenate(lane_blocks, axis=1)


def _even_post_kernel(x_ref, halo_ref, a_ref, m_ref, cw_ref, cb_ref, lg_ref, lb_ref, wa_ref, wm_ref,
                      o_ref, win_ref, *, per_seq):
    ts = a_ref.shape[0]
    first = (pl.program_id(0) % per_seq) == 0
    win_ref[0:CONV_HALO, :] = jnp.where(first, 0.0, halo_ref[...])
    win_ref[CONV_HALO:, :] = a_ref[...]
    y = _causal_conv(win_ref, cw_ref, ts) + cb_ref[...]
    mu = jnp.mean(y, axis=-1, keepdims=True)
    yc = y - mu
    var = jnp.mean(yc * yc, axis=-1, keepdims=True)
    act = _silu(yc * lax.rsqrt(var + EPS) * lg_ref[...] + lb_ref[...]).astype(BF16)
    o_ref[...] = x_ref[...] + _dot(act, wa_ref[...]) + _dot(m_ref[...], wm_ref[...])


def _even_post(x, a, m, conv_w, conv_b, ln_g, ln_b, w_out_a, w_out_m, ts, seq):
    T, D = x.shape
    C = a.shape[1]
    per_seq = seq // ts
    halo_blocks = ts // CONV_HALO
    row = lambda w: pl.BlockSpec((ts, w), lambda i: (i, 0))
    return pl.pallas_call(
        functools.partial(_even_post_kernel, per_seq=per_seq),
        grid=(T // ts,),
        in_specs=[row(D),
                  pl.BlockSpec((CONV_HALO, C), lambda i: (jnp.maximum(i * halo_blocks - 1, 0), 0)),
                  row(C), row(C), _const_spec((CONV_WIDTH, C)), _const_spec((1, C)),
                  _const_spec((1, C)), _const_spec((1, C)), _const_spec((C, D)), _const_spec((C, D))],
        out_specs=row(D),
        out_shape=jax.ShapeDtypeStruct((T, D), F32),
        scratch_shapes=[pltpu.VMEM((CONV_HALO + ts, C), F32)],
        compiler_params=_params("parallel"),
        name="even_post",
    )(x, a, a, m, conv_w, conv_b.reshape(1, C), ln_g.reshape(1, C), ln_b.reshape(1, C),
      w_out_a, w_out_m)


def _odd_pre_kernel(x_ref, g_ref, w_ref, cos_ref, sin_ref, q_ref, kt_ref, v_ref, gate_ref):
    xn = _rms(x_ref[...], g_ref[...]).astype(BF16)
    cos = cos_ref[...]
    sin = sin_ref[...]
    half = RET_QK // 2
    nqk = RET_HEADS * RET_QK
    k_scale = RET_QK ** -0.5
    nv = RET_HEADS * RET_V
    for h in range(RET_HEADS):
        sl = slice(h * RET_V, (h + 1) * RET_V)
        gate_ref[:, sl] = _silu(_dot(xn, w_ref[:, 2 * nqk + nv + h * RET_V:2 * nqk + nv + (h + 1) * RET_V]))
    for h in range(RET_HEADS):
        lo = h * RET_QK
        z = _dot(xn, w_ref[:, lo:lo + RET_QK])
        x1, x2 = z[:, :half], z[:, half:]
        q_ref[:, lo:lo + half] = (x1 * cos - x2 * sin).astype(BF16)
        q_ref[:, lo + half:lo + RET_QK] = (x1 * sin + x2 * cos).astype(BF16)
        z = _dot(xn, w_ref[:, nqk + lo:nqk + lo + RET_QK])
        x1, x2 = z[:, :half], z[:, half:]
        k = jnp.concatenate([x1 * cos - x2 * sin, x1 * sin + x2 * cos], axis=1) * k_scale
        kt_ref[0, lo:lo + RET_QK, :] = k.T.astype(BF16)
    for h in range(RET_HEADS):
        sl = slice(h * RET_V, (h + 1) * RET_V)
        v_ref[:, sl] = _dot(xn, w_ref[:, 2 * nqk + h * RET_V:2 * nqk + (h + 1) * RET_V]).astype(BF16)


def _odd_pre(x, gain, w_in, cos_r, sin_r, tm):
    T, D = x.shape
    nqk, nv = RET_HEADS * RET_QK, RET_HEADS * RET_V
    row = lambda w: pl.BlockSpec((tm, w), lambda i: (i, 0))
    return pl.pallas_call(
        _odd_pre_kernel,
        grid=(T // tm,),
        in_specs=[row(D), _const_spec((1, D)), _layer_spec(w_in), row(V7X_LANES), row(V7X_LANES)],
        out_specs=[row(nqk), pl.BlockSpec((1, nqk, tm), lambda i: (i, 0, 0)), row(nv), row(nv)],
        out_shape=[jax.ShapeDtypeStruct((T, nqk), BF16), jax.ShapeDtypeStruct((T // tm, nqk, tm), BF16),
                   jax.ShapeDtypeStruct((T, nv), BF16), jax.ShapeDtypeStruct((T, nv), F32)],
        compiler_params=_params("parallel"),
        name="odd_pre",
    )(x, gain.reshape(1, D), w_in.stack, cos_r, sin_r)


def _retention_kernel(x_ref, q_ref, kt_ref, v_ref, gate_ref, decay_ref, xi_ref, zeta_ref, gch_ref,
                      gng_ref, gnb_ref, wo_ref, o_ref, state_ref, y_ref):
    @pl.when(pl.program_id(1) == 0)
    def _():
        state_ref[...] = jnp.zeros(state_ref.shape, F32)

    ts = x_ref.shape[0]
    C = RET_BLOCK
    for c in range(ts // C):
        rows = slice(c * C, (c + 1) * C)
        for h in range(RET_HEADS):
            qk = slice(h * RET_QK, (h + 1) * RET_QK)
            vv = slice(h * RET_V, (h + 1) * RET_V)
            qc = q_ref[rows, qk]
            kt = kt_ref[0, qk, rows]
            vc = v_ref[rows, vv]
            state = state_ref[h]
            scores = (_dot(qc, kt) * decay_ref[h]).astype(BF16)
            out = _dot(scores, vc) + _dot(qc, state.astype(BF16)) * xi_ref[h]
            kz = (kt.astype(F32) * zeta_ref[h]).astype(BF16)
            state_ref[h] = state * gch_ref[h] + _dot(kz, vc)
            mu = jnp.mean(out, axis=-1, keepdims=True)
            oc = out - mu
            var = jnp.mean(oc * oc, axis=-1, keepdims=True)
            gn = oc * lax.rsqrt(var + EPS) * gng_ref[:, vv] + gnb_ref[:, vv]
            y_ref[rows, vv] = (gate_ref[rows, vv] * gn).astype(BF16)
    o_ref[...] = x_ref[...] + _dot(y_ref[...], wo_ref[...])


def _retention(x, q, k, v, gate, decay, xi, zeta, gch, gn_g, gn_b, w_out, batch, seq, ts):
    T, D = x.shape
    nqk, nv = RET_HEADS * RET_QK, RET_HEADS * RET_V
    per_seq = seq // ts
    row = lambda w: pl.BlockSpec((ts, w), lambda b, i: (b * per_seq + i, 0))
    return pl.pallas_call(
        _retention_kernel,
        grid=(batch, per_seq),
        in_specs=[row(D), row(nqk), pl.BlockSpec((1, nqk, ts), lambda b, i: (b * per_seq + i, 0, 0)),
                  row(nv), row(nv), _const_spec(decay.shape), _const_spec(xi.shape), _const_spec(zeta.shape),
                  _const_spec(gch.shape), _const_spec((1, nv)), _const_spec((1, nv)),
                  _layer_spec(w_out)],
        out_specs=row(D),
        out_shape=jax.ShapeDtypeStruct((T, D), F32),
        scratch_shapes=[pltpu.VMEM((RET_HEADS, RET_QK, RET_V), F32), pltpu.VMEM((ts, nv), BF16)],
        compiler_params=_params("parallel", "arbitrary"),
        name="retention",
    )(x, q, k, v, gate, decay, xi, zeta, gch, gn_g.reshape(1, nv), gn_b.reshape(1, nv), w_out.stack)


def _retention_constants():
    H, C = RET_HEADS, RET_BLOCK
    log_g = jnp.log1p(-jnp.exp2(-5.0 - jnp.arange(H, dtype=F32)))
    idx = jnp.arange(C, dtype=F32)
    diff = idx[:, None] - idx[None, :]
    decay = jnp.where(diff >= 0, jnp.exp(log_g[:, None, None] * jnp.maximum(diff, 0.0)), 0.0)
    xi = jnp.exp(log_g[:, None] * (idx + 1.0))[:, :, None]
    zeta = jnp.exp(log_g[:, None] * (C - 1.0 - idx))[:, None, :]
    gch = jnp.exp(log_g * C)[:, None, None]
    return decay, xi, zeta, gch


def _swap_halves(w):
    half = w.shape[-1] // 2
    return jnp.concatenate([w[..., half:], w[..., :half]], axis=-1)


def _mla_weights(w_in, w_q_b, w_kv_b, q_nope_norm, k_nope_norm, q_rope_norm, k_rope_norm):
    D = w_in.shape[0]
    pad = MLA_HEAD_PAD - QK_NOPE - QK_ROPE
    z = lambda *s: jnp.zeros(s, F32)
    base = 2 * CONV_DIM + Q_LORA + KV_LORA
    w_kr = w_in[:, base:base + QK_ROPE]
    grp = lambda w: jnp.concatenate([z(D, QK_NOPE), w, z(D, pad)], axis=1)
    w_in_p = jnp.concatenate([w_in[:, :base], grp(w_kr), grp(_swap_halves(w_kr))], axis=1)

    wq = w_q_b.reshape(Q_LORA, MLA_HEADS, QK_NOPE + QK_ROPE)
    wq_nope, wq_rope = wq[..., :QK_NOPE], wq[..., QK_NOPE:]
    zq = z(Q_LORA, MLA_HEADS, pad)
    wq_p = jnp.concatenate([wq_nope, wq_rope, zq], axis=-1).reshape(Q_LORA, -1)
    wq_sw = jnp.concatenate([jnp.zeros_like(wq_nope), _swap_halves(wq_rope), zq], axis=-1).reshape(Q_LORA, -1)
    wq_all = jnp.concatenate([wq_p, wq_sw], axis=1)

    wkv = w_kv_b.reshape(KV_LORA, MLA_HEADS, QK_NOPE + V_HEAD)
    wk_p = jnp.concatenate([wkv[..., :QK_NOPE], z(KV_LORA, MLA_HEADS, MLA_HEAD_PAD - QK_NOPE)],
                           axis=-1).reshape(KV_LORA, -1)
    wv_p = wkv[..., QK_NOPE:].reshape(KV_LORA, -1)

    lane = jnp.arange(MLA_HEAD_PAD)
    in_nope = lane < QK_NOPE
    in_rope = (lane >= QK_NOPE) & (lane < QK_NOPE + QK_ROPE)
    seg = (jnp.where(in_nope[:, None] & in_nope[None, :], 1.0 / QK_NOPE, 0.0)
           + jnp.where(in_rope[:, None] & in_rope[None, :], 1.0 / QK_ROPE, 0.0))
    vec = lambda nope, rope: jnp.concatenate([nope, rope, z(pad)]).reshape(1, MLA_HEAD_PAD)
    pair = lambda v: jnp.concatenate([v, v], axis=1)
    gq = pair(vec(q_nope_norm, q_rope_norm))
    gqs = pair(vec(jnp.zeros_like(q_nope_norm), _swap_halves(q_rope_norm)))
    gk = pair(vec(k_nope_norm, z(QK_ROPE)))
    gkr = vec(z(QK_NOPE), k_rope_norm)
    gkrs = vec(z(QK_NOPE), _swap_halves(k_rope_norm))
    zs = jnp.zeros_like(seg)
    seg2 = jnp.concatenate([jnp.concatenate([seg, zs], axis=1), jnp.concatenate([zs, seg], axis=1)], axis=0)
    return (w_in_p.astype(BF16), wq_all.astype(BF16), wk_p.astype(BF16), wv_p.astype(BF16),
            seg2.astype(BF16), gq, gqs, gk, gkr, gkrs)


def kernel(x, mem, positions, ffn1_norm, ffn1_w_gate, ffn1_w_up, ffn1_w_down, ffn2_norm, ffn2_w_gate, ffn2_w_up, ffn2_w_down, mix_norm, xattn_norm, mem_norm, xattn_wq, xattn_wk, xattn_wv, xattn_wo, xattn_q_norm, xattn_k_norm, ev_w_in, ev_conv_w, ev_conv_b, ev_conv_ln_g, ev_conv_ln_b, ev_q_a_norm, ev_w_q_b, ev_kv_a_norm, ev_w_kv_b, ev_q_nope_norm, ev_k_nope_norm, ev_q_rope_norm, ev_k_rope_norm, ev_w_out, od_w_in, od_gn_g, od_gn_b, od_w_out):
    B, S, D = x.shape
    depth = ffn1_norm.shape[0]
    tm = min(TOKEN_TILE, S)
    tf = min(FFN_TOKEN_TILE, B * S)
    te = min(FFN_TOKEN_TILE, S)
    assert S % tm == 0 and (B * S) % tf == 0 and S % min(tf, S) == 0 and S % te == 0 and te % tm == 0
    assert tm % RET_BLOCK == 0 and tm % CONV_HALO == 0 and te % CONV_ROWS == 0
    bf = lambda w: w.astype(BF16)

    h = x.reshape(B * S, D)
    cos_r, sin_r, cm, sm = _rope_tables(positions, math.gcd(B * S, ROPE_TOKEN_TILE))
    decay, xi, zeta, gch = _retention_constants()
    f1_gate, f1_up, f1_down = bf(ffn1_w_gate), bf(ffn1_w_up), bf(ffn1_w_down)
    f2_gate, f2_up, f2_down = bf(ffn2_w_gate), bf(ffn2_w_up), bf(ffn2_w_down)
    x_wq, x_wk, x_wv, x_wo = bf(xattn_wq), bf(xattn_wk), bf(xattn_wv), bf(xattn_wo)
    od_in, od_out = bf(od_w_in), bf(od_w_out)

    for l in range(depth):
        h = _ffn(h, ffn1_norm[l], _Layer(f1_gate, l), _Layer(f1_up, l), _Layer(f1_down, l), tf)
        if l % 2 == 0:
            e = l // 2
            (w_in_p, wq_p, wk_p, wv_p, seg, gq, gqs, gk, gkr, gkrs) = _mla_weights(
                ev_w_in[e], ev_w_q_b[e], ev_w_kv_b[e], ev_q_nope_norm[e], ev_k_nope_norm[e],
                ev_q_rope_norm[e], ev_k_rope_norm[e])
            a, q, k, v = _even_pre(h, mix_norm[l], w_in_p, cm, sm, ev_q_a_norm[e].reshape(1, -1), wq_p,
                                   ev_kv_a_norm[e].reshape(1, -1), wk_p, wv_p, seg, gq, gqs, gk, gkr,
                                   gkrs, te, tm)
            m = _mla_attention(q, k, v, B, S, tm)
            h = _even_post(h, a, m, ev_conv_w[e], ev_conv_b[e], ev_conv_ln_g[e], ev_conv_ln_b[e],
                           bf(ev_w_out[e][:CONV_DIM]), bf(ev_w_out[e][CONV_DIM:]), te, S)
        else:
            o = l // 2
            q, k, v, gate = _odd_pre(h, mix_norm[l], _Layer(od_in, o), cos_r, sin_r, tm)
            h = _retention(h, q, k, v, gate, decay, xi, zeta, gch, od_gn_g[o], od_gn_b[o],
                           _Layer(od_out, o), B, S, tm)
        kt, vw = _memkv(mem, mem_norm[l], _Layer(x_wk, l), _Layer(x_wv, l), _Layer(x_wo, l), xattn_k_norm[l])
        h = _xattn(h, xattn_norm[l], _Layer(x_wq, l), xattn_q_norm[l], kt, vw, min(tf, S), S)
        h = _ffn(h, ffn2_norm[l], _Layer(f2_gate, l), _Layer(f2_up, l), _Layer(f2_down, l), tf)
    return h.reshape(B, S, D)
```
